```python
import functools
import jax, jax.numpy as jnp
from jax import lax
import numpy as np

D_MODEL = 1024
BATCH = 1
SEQ = 16384
DEPTH = 1

CHUNK = 64
N_META = 16
LEAD = CHUNK - N_META
Q_BLOCK = 128

FOX_HEADS = 8
FOX_HEAD_DIM = 64
FOX_WIDTH = FOX_HEADS * FOX_HEAD_DIM

GDN_HEADS = 8
GDN_KEY_DIM = 128
GDN_VAL_DIM = 128
GDN_K_WIDTH = GDN_HEADS * GDN_KEY_DIM
GDN_V_WIDTH = GDN_HEADS * GDN_VAL_DIM
GDN_CONV_CH = 2 * GDN_K_WIDTH + GDN_V_WIDTH
CONV_WIDTH = 4

D_FF = ((8 * D_MODEL + 3 * 256 - 1) // (3 * 256)) * 256

IN_SPLIT_SIZES = (FOX_WIDTH, FOX_WIDTH, FOX_WIDTH, FOX_HEADS,
                  GDN_K_WIDTH, GDN_K_WIDTH, GDN_V_WIDTH,
                  GDN_HEADS, GDN_HEADS, GDN_V_WIDTH,
                  D_MODEL, D_MODEL)
IN_WIDTH = sum(IN_SPLIT_SIZES)

ALPHA = (2.0 * DEPTH) ** 0.25
BETA = (8.0 * DEPTH) ** -0.25
LN_EPS = 1e-5
NORM_EPS = 1e-6

kernel_name = 'hybrid_fox_gdn_deepnorm_block'


def _layer_norm(x, g, b):
    xf = x.astype(jnp.float32)
    mu = jnp.mean(xf, axis=-1, keepdims=True)
    var = jnp.mean(jnp.square(xf - mu), axis=-1, keepdims=True)
    return ((xf - mu) * lax.rsqrt(var + LN_EPS) * g.astype(jnp.float32) + b.astype(jnp.float32)).astype(x.dtype)


def _l2norm(x):
    return x * lax.rsqrt(jnp.sum(x * x, axis=-1, keepdims=True) + NORM_EPS)


def _causal_depthwise_conv(x, w):
    return lax.conv_general_dilated(
        x, w[:, None, :].astype(x.dtype), window_strides=(1,),
        padding=[(CONV_WIDTH - 1, 0)], dimension_numbers=('NWC', 'WIO', 'NWC'),
        feature_group_count=x.shape[-1])


def _forgetting_attention(q, k, v, log_f):
    length, head_dim = q.shape[2], q.shape[3]
    cum = jnp.cumsum(log_f, axis=-1)
    scale = head_dim ** -0.5
    k_pos = jnp.arange(length)

    def query_block(i):
        start = i * Q_BLOCK
        q_blk = lax.dynamic_slice_in_dim(q, start, Q_BLOCK, axis=2)
        c_blk = lax.dynamic_slice_in_dim(cum, start, Q_BLOCK, axis=2)
        q_pos = start + jnp.arange(Q_BLOCK)
        logits = jnp.einsum('bhqd,bhkd->bhqk', q_blk, k).astype(jnp.float32) * scale
        logits = logits + c_blk[..., :, None] - cum[..., None, :]
        visible = (k_pos[None, :] <= q_pos[:, None]) & (
            (k_pos[None, :] >= LEAD) | (k_pos[None, :] == q_pos[:, None]))
        probs = jax.nn.softmax(jnp.where(visible, logits, -jnp.inf), axis=-1)
        return jnp.einsum('bhqk,bhkd->bhqd', probs.astype(v.dtype), v)

    out = lax.map(query_block, jnp.arange(length // Q_BLOCK))
    return jnp.moveaxis(out, 0, 2).reshape(q.shape[0], q.shape[1], length, v.shape[-1])


def _gated_delta_rule(q, k, v, g, beta):
    b, h, length, dk = q.shape
    dv = v.shape[-1]
    n = length // CHUNK
    q = q * dk ** -0.5
    q, k, v = (t.reshape(b, h, n, CHUNK, t.shape[-1]) for t in (q, k, v))
    g, beta = (t.reshape(b, h, n, CHUNK) for t in (g, beta))
    gc = jnp.cumsum(g, axis=-1)
    idx = jnp.arange(CHUNK)
    incl = idx[:, None] >= idx[None, :]
    strict = idx[:, None] > idx[None, :]
    decay = jnp.exp(jnp.where(incl, gc[..., :, None] - gc[..., None, :], -jnp.inf))
    k_beta = k * beta[..., None]
    a_strict = jnp.where(strict, jnp.einsum('bhnid,bhnjd->bhnij', k_beta, k) * decay, 0.0)
    solve = functools.partial(lax.linalg.triangular_solve, left_side=True, lower=True, unit_diagonal=True)
    u = solve(a_strict, v * beta[..., None])
    w = solve(a_strict, k_beta * jnp.exp(gc)[..., None])
    attn = jnp.einsum('bhnid,bhnjd->bhnij', q, k) * decay
    q_dec = q * jnp.exp(gc)[..., None]
    k_dec = k * jnp.exp(gc[..., -1:] - gc)[..., None]
    g_last = jnp.exp(gc[..., -1])

    def step(state, xs):
        u_c, w_c, q_c, k_c, a_c, gl = xs
        v_new = u_c - jnp.einsum('bhcd,bhde->bhce', w_c, state)
        o_c = jnp.einsum('bhcd,bhde->bhce', q_c, state) + jnp.einsum('bhij,bhje->bhie', a_c, v_new)
        state = state * gl[..., None, None] + jnp.einsum('bhcd,bhce->bhde', k_c, v_new)
        return state, o_c

    xs = tuple(jnp.moveaxis(t, 2, 0) for t in (u, w, q_dec, k_dec, attn, g_last))
    state0 = jnp.zeros((b, h, dk, dv), jnp.float32)
    _, o = lax.scan(step, state0, xs)
    return jnp.moveaxis(o, 0, 2).reshape(b, h, length, dv)


def _hybrid_layer(h, w_in, b_f, conv_w, a_log, dt_bias, gdn_norm_w, w_out_a, w_out_b, w_o,
                  ln1_g, ln1_b, w_gate, w_up, w_down, ln2_g, ln2_b):
    f32 = jnp.float32
    bsz, length, _ = h.shape
    slot_mask = (jnp.arange(length) >= LEAD).astype(h.dtype)
    proj = (h @ w_in) * slot_mask[None, :, None]
    (q_a, k_a, v_a, f_a, q_b, k_b, v_b, a_b, b_b, z_b, gate_a, gate_b) = jnp.split(
        proj, np.cumsum(IN_SPLIT_SIZES)[:-1].tolist(), axis=-1)

    def heads(t, n_heads, dim):
        return t.reshape(bsz, length, n_heads, dim).transpose(0, 2, 1, 3)

    log_f = jax.nn.log_sigmoid(f_a.astype(f32) + b_f.astype(f32)).transpose(0, 2, 1)
    y_a = _forgetting_attention(heads(q_a, FOX_HEADS, FOX_HEAD_DIM), heads(k_a, FOX_HEADS, FOX_HEAD_DIM),
                                heads(v_a, FOX_HEADS, FOX_HEAD_DIM), log_f)
    y_a = y_a.transpose(0, 2, 1, 3).reshape(bsz, length, FOX_WIDTH)

    qkv = jax.nn.silu(_causal_depthwise_conv(jnp.concatenate([q_b, k_b, v_b], axis=-1), conv_w))
    q_c, k_c, v_c = jnp.split(qkv, [GDN_K_WIDTH, 2 * GDN_K_WIDTH], axis=-1)
    q_c = _l2norm(heads(q_c, GDN_HEADS, GDN_KEY_DIM).astype(f32))
    k_c = _l2norm(heads(k_c, GDN_HEADS, GDN_KEY_DIM).astype(f32))
    v_c = heads(v_c, GDN_HEADS, GDN_VAL_DIM).astype(f32)
    beta = jax.nn.sigmoid(b_b.astype(f32)).transpose(0, 2, 1)
    g = (-jnp.exp(a_log.astype(f32)) * jax.nn.softplus(a_b.astype(f32) + dt_bias.astype(f32))).transpose(0, 2, 1)
    o = _gated_delta_rule(q_c, k_c, v_c, g, beta).transpose(0, 2, 1, 3)
    o = o * lax.rsqrt(jnp.mean(jnp.square(o), axis=-1, keepdims=True) + NORM_EPS) * gdn_norm_w.astype(f32)
    o = o * jax.nn.silu(z_b.astype(f32).reshape(bsz, length, GDN_HEADS, GDN_VAL_DIM))
    y_b = o.reshape(bsz, length, GDN_V_WIDTH).astype(h.dtype)

    mixed = jax.nn.sigmoid(gate_a) * (y_a @ w_out_a) + jax.nn.sigmoid(gate_b) * (y_b @ w_out_b)
    h = _layer_norm(ALPHA * h + mixed @ w_o, ln1_g, ln1_b)

    ffn = (jax.nn.silu(h @ w_gate) * (h @ w_up)) @ w_down
    return _layer_norm(ALPHA * h + ffn, ln2_g, ln2_b)


def setup_inputs(seed: int = 0) -> dict:
    key = jax.random.key(seed)
    ks = jax.random.split(key, 20)

    def nrm(k, shape, scale):
        return jax.random.normal(k, shape, jnp.float32) * scale

    dt = jnp.exp(jax.random.uniform(ks[8], (DEPTH, GDN_HEADS), jnp.float32,
                                    minval=float(np.log(1e-3)), maxval=float(np.log(1e-1))))
    return {
        'x': nrm(ks[0], (BATCH, SEQ, D_MODEL), 1.0),
        'meta_tokens': nrm(ks[1], (N_META, D_MODEL), 1.0),
        'ln_in_g': 1.0 + nrm(ks[2], (D_MODEL,), 0.02),
        'ln_in_b': nrm(ks[3], (D_MODEL,), 0.02),
        'w_in': nrm(ks[4], (DEPTH, D_MODEL, IN_WIDTH), D_MODEL ** -0.5),
        'b_f': 3.0 + nrm(ks[5], (DEPTH, FOX_HEADS), 0.1),
        'conv_w': nrm(ks[6], (DEPTH, CONV_WIDTH, GDN_CONV_CH), CONV_WIDTH ** -0.5),
        'a_log': jnp.log(jax.random.uniform(ks[7], (DEPTH, GDN_HEADS), jnp.float32, minval=1.0, maxval=16.0)),
        'dt_bias': dt + jnp.log(-jnp.expm1(-dt)),
        'gdn_norm_w': 1.0 + nrm(ks[9], (DEPTH, GDN_VAL_DIM), 0.02),
        'w_out_a': nrm(ks[10], (DEPTH, FOX_WIDTH, D_MODEL), FOX_WIDTH ** -0.5),
        'w_out_b': nrm(ks[11], (DEPTH, GDN_V_WIDTH, D_MODEL), GDN_V_WIDTH ** -0.5),
        'w_o': nrm(ks[12], (DEPTH, D_MODEL, D_MODEL), D_MODEL ** -0.5 * BETA),
        'ln1_g': 1.0 + nrm(ks[13], (DEPTH, D_MODEL), 0.02),
        'ln1_b': nrm(ks[14], (DEPTH, D_MODEL), 0.02),
        'w_gate': nrm(ks[15], (DEPTH, D_MODEL, D_FF), D_MODEL ** -0.5),
        'w_up': nrm(ks[16], (DEPTH, D_MODEL, D_FF), D_MODEL ** -0.5),
        'w_down': nrm(ks[17], (DEPTH, D_FF, D_MODEL), D_FF ** -0.5 * BETA),
        'ln2_g': 1.0 + nrm(ks[18], (DEPTH, D_MODEL), 0.02),
        'ln2_b': nrm(ks[19], (DEPTH, D_MODEL), 0.02),
    }


def reference(x, meta_tokens, ln_in_g, ln_in_b, w_in, b_f, conv_w, a_log, dt_bias, gdn_norm_w,
              w_out_a, w_out_b, w_o, ln1_g, ln1_b, w_gate, w_up, w_down, ln2_g, ln2_b):
    bsz, seq, d = x.shape
    length = ((CHUNK + seq + Q_BLOCK - 1) // Q_BLOCK) * Q_BLOCK
    tail = length - CHUNK - seq
    h = jnp.concatenate([
        jnp.zeros((bsz, LEAD, d), x.dtype),
        jnp.broadcast_to(meta_tokens.astype(x.dtype)[None], (bsz, N_META, d)),
        x,
        jnp.zeros((bsz, tail, d), x.dtype)], axis=1)
    h = _layer_norm(h, ln_in_g, ln_in_b)
    for l in range(DEPTH):
        h = _hybrid_layer(h, w_in[l], b_f[l], conv_w[l], a_log[l], dt_bias[l], gdn_norm_w[l],
                          w_out_a[l], w_out_b[l], w_o[l], ln1_g[l], ln1_b[l],
                          w_gate[l], w_up[l], w_down[l], ln2_g[l], ln2_b[l])
    return h[:, CHUNK:CHUNK + seq]
```

```python
import functools
import math

import jax
import jax.numpy as jnp
import numpy as np
from jax import lax
from jax.experimental import pallas as pl
from jax.experimental.pallas import tpu as pltpu

D_MODEL = 1024
CHUNK = 64
N_META = 16
FOX_HEADS = 8
FOX_HEAD_DIM = 64
FOX_WIDTH = FOX_HEADS * FOX_HEAD_DIM
GDN_HEADS = 8
GDN_KEY_DIM = 128
GDN_VAL_DIM = 128
GDN_K_WIDTH = GDN_HEADS * GDN_KEY_DIM
GDN_V_WIDTH = GDN_HEADS * GDN_VAL_DIM
CONV_WIDTH = 4
DEPTH = 1
ALPHA = (2.0 * DEPTH) ** 0.25
LN_EPS = 1e-5
NORM_EPS = 1e-6

LANES = 128
SUBLANES = 8
VMEM_LIMIT_BYTES = 56 * 1024 * 1024

ROW_BLOCK = 512
PRE = ROW_BLOCK
LEAD = PRE - N_META
CHUNKS_PER_BLOCK = ROW_BLOCK // CHUNK
HEAD_SLOT = LANES
BIAS_LANE = FOX_HEAD_DIM
MASK_BIAS = -1e30
NEG_INIT = -1e30
F32 = jnp.float32
BF16 = jnp.bfloat16

LOGF_LANE0 = 0
G_LANE0 = 8
BETA_LANE0 = 16


def _nt_dot(a, b):
    return lax.dot_general(a, b, (((1,), (1,)), ((), ())), preferred_element_type=F32)


def _dot(a, b):
    return jnp.dot(a, b, preferred_element_type=F32)


def _layer_norm(x, g, b):
    mu = jnp.mean(x, axis=-1, keepdims=True)
    xc = x - mu
    var = jnp.mean(xc * xc, axis=-1, keepdims=True)
    return xc * lax.rsqrt(var + LN_EPS) * g + b


def _softplus(x):
    return jnp.maximum(x, 0.0) + jnp.log(1.0 + jnp.exp(-jnp.abs(x)))


def _sigmoid(x):
    return 1.0 / (1.0 + jnp.exp(-x))


def _silu(x):
    return x * _sigmoid(x)


def _split3(x):
    hi = x.astype(BF16)
    r1 = x - hi.astype(F32)
    mid = r1.astype(BF16)
    lo = (r1 - mid.astype(F32)).astype(BF16)
    return hi, mid, lo


def _resident(shape):
    nd = len(shape)
    return pl.BlockSpec(shape, lambda *_: (0,) * nd, pipeline_mode=pl.Buffered(1))


def _proj_body(prefix_ref, x_ref, lng_ref, lnb_ref, wq_ref, wk_ref, wv_ref, wg_ref, wz_ref, wga_ref,
               wgb_ref, wn_ref, par_ref, tri_ref, e_ref, qone_ref, vone_ref,
               qa_ref, ka_ref, va_ref, qkv_ref, z_ref, ga_ref, gb_ref, small_ref, r_ref,
               carry_ref):
    i = pl.program_id(0)

    @pl.when(i == 0)
    def _():
        carry_ref[...] = jnp.zeros_like(carry_ref)

    src = jnp.where(i == 0, prefix_ref[...], x_ref[...])
    h = _layer_norm(src, lng_ref[...], lnb_ref[...])
    row = lax.broadcasted_iota(jnp.int32, (ROW_BLOCK, 1), 0)
    valid = jnp.logical_or(i > 0, row >= LEAD)
    hb = jnp.where(valid, h, 0.0).astype(BF16)

    nar = _dot(hb, wn_ref[...])
    lane = lax.broadcasted_iota(jnp.int32, (1, LANES), 1)
    b_f = par_ref[0:1, :]
    a_log = par_ref[1:2, :]
    dt_bias = par_ref[2:3, :]
    xf = nar + b_f
    logf = jnp.minimum(xf, 0.0) - jnp.log(1.0 + jnp.exp(-jnp.abs(xf)))
    g = -jnp.exp(a_log) * _softplus(nar + dt_bias)
    beta = _sigmoid(nar)
    is_logf = lane < G_LANE0
    is_g = jnp.logical_and(lane >= G_LANE0, lane < BETA_LANE0)
    vals = jnp.where(is_logf, logf, jnp.where(is_g, g, 0.0))

    hi, mid, lo = _split3(vals)
    tri = tri_ref[...]
    cs = _dot(tri, hi) + _dot(tri, mid) + _dot(tri, lo)
    d_blk = cs[:ROW_BLOCK]
    gc = cs[ROW_BLOCK:]
    carry = carry_ref[...]
    r_ref[...] = jnp.broadcast_to(carry, r_ref.shape)
    carry_ref[...] = carry + d_blk[ROW_BLOCK - 1:ROW_BLOCK, :]
    small_ref[...] = jnp.where(is_g, gc, jnp.where(lane >= BETA_LANE0, beta, d_blk))

    nd = jnp.where(valid, -d_blk, MASK_BIAS)
    nd = jnp.where(is_logf, nd, 0.0)
    phi, pmid, plo = _split3(nd)
    pieces = jnp.concatenate([phi, pmid, plo], axis=1)
    qa_ref[...] = (_dot(hb, wq_ref[...]) + qone_ref[...]).astype(BF16)
    ka_ref[...] = (_dot(hb, wk_ref[...]) + _dot(pieces, e_ref[...])).astype(BF16)
    va_ref[...] = (_dot(hb, wv_ref[...]) + vone_ref[...]).astype(BF16)

    for c in range(3):
        sl = slice(c * GDN_K_WIDTH, (c + 1) * GDN_K_WIDTH)
        qkv_ref[:, sl] = _dot(hb, wg_ref[:, sl]).astype(BF16)
    z_ref[...] = _dot(hb, wz_ref[...]).astype(BF16)
    ga_ref[...] = _dot(hb, wga_ref[...]).astype(BF16)
    gb_ref[...] = _dot(hb, wgb_ref[...]).astype(BF16)


def _proj_call(prefix, x, lng, lnb, wq, wk, wv, wg, wz, wga, wgb, wn, par, tri, e_mat, qone, vone):
    seq = x.shape[0]
    n_blocks = seq // ROW_BLOCK + 1
    rows = n_blocks * ROW_BLOCK
    wide = FOX_HEADS * HEAD_SLOT

    def row_spec(width):
        return pl.BlockSpec((ROW_BLOCK, width), lambda i: (i, 0))

    in_specs = [
        _resident(prefix.shape),
        pl.BlockSpec((ROW_BLOCK, D_MODEL), lambda i: (jnp.maximum(i - 1, 0), 0)),
        _resident(lng.shape), _resident(lnb.shape),
        _resident(wq.shape), _resident(wk.shape), _resident(wv.shape), _resident(wg.shape),
        _resident(wz.shape), _resident(wga.shape), _resident(wgb.shape), _resident(wn.shape),
        _resident(par.shape), _resident(tri.shape), _resident(e_mat.shape),
        _resident(qone.shape), _resident(vone.shape),
    ]
    out_shape = [
        jax.ShapeDtypeStruct((rows, wide), BF16),
        jax.ShapeDtypeStruct((rows, wide), BF16),
        jax.ShapeDtypeStruct((rows, wide), BF16),
        jax.ShapeDtypeStruct((rows, 3 * GDN_K_WIDTH), BF16),
        jax.ShapeDtypeStruct((rows, GDN_V_WIDTH), BF16),
        jax.ShapeDtypeStruct((rows, D_MODEL), BF16),
        jax.ShapeDtypeStruct((rows, D_MODEL), BF16),
        jax.ShapeDtypeStruct((rows, LANES), F32),
        jax.ShapeDtypeStruct((n_blocks * SUBLANES, LANES), F32),
    ]
    out_specs = [row_spec(wide), row_spec(wide), row_spec(wide), row_spec(3 * GDN_K_WIDTH),
                 row_spec(GDN_V_WIDTH), row_spec(D_MODEL), row_spec(D_MODEL), row_spec(LANES),
                 pl.BlockSpec((SUBLANES, LANES), lambda i: (i, 0))]
    return pl.pallas_call(
        _proj_body,
        grid=(n_blocks,),
        in_specs=in_specs,
        out_specs=out_specs,
        out_shape=out_shape,
        scratch_shapes=[pltpu.VMEM((1, LANES), F32)],
        compiler_params=pltpu.CompilerParams(dimension_semantics=("arbitrary",),
                                             vmem_limit_bytes=VMEM_LIMIT_BYTES),
        name="ln_in_proj",
    )(prefix, x, lng, lnb, wq, wk, wv, wg, wz, wga, wgb, wn, par, tri, e_mat, qone, vone)


def _attn_body(r_ref, q_ref, k_ref, v_ref, o_ref):
    h = pl.program_id(0)
    i = pl.program_id(1)
    q = q_ref[...]

    def step(j, carry, causal):
        m, acc = carry
        start = pl.multiple_of(j * ROW_BLOCK, ROW_BLOCK)
        kb = k_ref[pl.ds(start, ROW_BLOCK), :]
        vb = v_ref[pl.ds(start, ROW_BLOCK), :]
        s = _nt_dot(q, kb)
        if causal:
            rr = lax.broadcasted_iota(jnp.int32, (ROW_BLOCK, ROW_BLOCK), 0)
            cc = lax.broadcasted_iota(jnp.int32, (ROW_BLOCK, ROW_BLOCK), 1)
            s = jnp.where(rr >= cc, s, MASK_BIAS)
        rj = r_ref[j * SUBLANES + h]
        m_new = jnp.maximum(m, jnp.max(s, axis=1, keepdims=True) - rj)
        p = jnp.exp(s - (m_new + rj))
        alpha = jnp.exp(m - m_new)
        acc = alpha * acc + _dot(p.astype(BF16), vb)
        return m_new, acc

    m0 = jnp.full((ROW_BLOCK, 1), NEG_INIT, F32)
    acc0 = jnp.zeros((ROW_BLOCK, HEAD_SLOT), F32)
    m, acc = lax.fori_loop(0, i + 1, lambda j, c: step(j, c, False), (m0, acc0))
    m, acc = step(i + 1, (m, acc), True)
    denom = acc[:, FOX_HEAD_DIM:FOX_HEAD_DIM + 1]
    o_ref[...] = (acc / denom).astype(o_ref.dtype)


def _attn_call(r_flat, qa, ka, va):
    rows = qa.shape[0]
    n_q = rows // ROW_BLOCK - 1
    return pl.pallas_call(
        _attn_body,
        grid=(FOX_HEADS, n_q),
        in_specs=[
            pl.BlockSpec(memory_space=pltpu.SMEM),
            pl.BlockSpec((ROW_BLOCK, HEAD_SLOT), lambda h, i: (i + 1, h)),
            pl.BlockSpec((rows, HEAD_SLOT), lambda h, i: (0, h)),
            pl.BlockSpec((rows, HEAD_SLOT), lambda h, i: (0, h)),
        ],
        out_specs=pl.BlockSpec((ROW_BLOCK, HEAD_SLOT), lambda h, i: (i, h)),
        out_shape=jax.ShapeDtypeStruct((n_q * ROW_BLOCK, FOX_HEADS * HEAD_SLOT), BF16),
        compiler_params=pltpu.CompilerParams(dimension_semantics=("arbitrary", "arbitrary"),
                                             vmem_limit_bytes=VMEM_LIMIT_BYTES),
        name="fox_attention",
    )(r_flat, qa, ka, va)


def _gdn_body(qkv_ref, hist_ref, small_ref, z_ref, convw_ref, normw_ref, y_ref,
              state_ref, xn_ref):
    i = pl.program_id(0)

    @pl.when(i == 0)
    def _():
        state_ref[...] = jnp.zeros_like(state_ref)

    hist = jnp.where(i > 0, hist_ref[...].astype(F32), 0.0)
    cur = qkv_ref[...].astype(F32)
    ext = jnp.concatenate([hist, cur], axis=0)
    conv = jnp.zeros_like(cur)
    for t in range(CONV_WIDTH):
        off = SUBLANES - (CONV_WIDTH - 1) + t
        conv = conv + convw_ref[t:t + 1, :] * ext[off:off + ROW_BLOCK, :]
    act = _silu(conv)
    for hd in range(2 * GDN_HEADS):
        sl = slice(hd * GDN_KEY_DIM, (hd + 1) * GDN_KEY_DIM)
        blk = act[:, sl]
        scale = lax.rsqrt(jnp.sum(blk * blk, axis=-1, keepdims=True) + NORM_EPS)
        if hd < GDN_HEADS:
            scale = scale * (GDN_KEY_DIM ** -0.5)
        xn_ref[:, sl] = blk * scale
    xn_ref[:, 2 * GDN_K_WIDTH:] = act[:, 2 * GDN_K_WIDTH:]

    ri = lax.broadcasted_iota(jnp.int32, (CHUNK, CHUNK), 0)
    ci = lax.broadcasted_iota(jnp.int32, (CHUNK, CHUNK), 1)
    incl = ri >= ci
    strict = ri > ci
    eye = (ri == ci).astype(F32)
    normw = normw_ref[...]

    def chunk_step(c, _):
        r0 = pl.multiple_of(c * CHUNK, CHUNK)
        sm = small_ref[pl.ds(r0, CHUNK), :]
        sm_t = sm.T
        for hd in range(GDN_HEADS):
            qs = slice(hd * GDN_KEY_DIM, (hd + 1) * GDN_KEY_DIM)
            ks = slice(GDN_K_WIDTH + hd * GDN_KEY_DIM, GDN_K_WIDTH + (hd + 1) * GDN_KEY_DIM)
            vs = slice(2 * GDN_K_WIDTH + hd * GDN_VAL_DIM, 2 * GDN_K_WIDTH + (hd + 1) * GDN_VAL_DIM)
            qc = xn_ref[pl.ds(r0, CHUNK), qs]
            kc = xn_ref[pl.ds(r0, CHUNK), ks]
            vc = xn_ref[pl.ds(r0, CHUNK), vs]
            gcol = sm[:, G_LANE0 + hd:G_LANE0 + hd + 1]
            bcol = sm[:, BETA_LANE0 + hd:BETA_LANE0 + hd + 1]
            grow = sm_t[G_LANE0 + hd:G_LANE0 + hd + 1, :]
            glast = gcol[CHUNK - 1:CHUNK, :]
            decay = jnp.exp(jnp.where(incl, gcol - grow, -jnp.inf))
            egc = jnp.exp(gcol)
            kb = kc * bcol
            kcb = kc.astype(BF16)
            a_mat = jnp.where(strict, _nt_dot(kb.astype(BF16), kcb) * decay, 0.0)
            pw = -a_mat
            t_inv = eye + pw
            for _k in range(int(math.log2(CHUNK)) - 1):
                pwb = pw.astype(BF16)
                pw = _dot(pwb, pwb)
                t_inv = t_inv + _dot(t_inv.astype(BF16), pw.astype(BF16))
            tb = t_inv.astype(BF16)
            u = _dot(tb, (vc * bcol).astype(BF16))
            w = _dot(tb, (kb * egc).astype(BF16))
            attn = jnp.where(incl, _nt_dot(qc.astype(BF16), kcb) * decay, 0.0)
            q_dec = (qc * egc).astype(BF16)
            k_dec = (kc * jnp.exp(glast - gcol)).astype(BF16)
            state = state_ref[hd]
            sb = state.astype(BF16)
            v_new = u - _dot(w.astype(BF16), sb)
            vnb = v_new.astype(BF16)
            o = _dot(q_dec, sb) + _dot(attn.astype(BF16), vnb)
            state_ref[hd] = state * jnp.exp(glast) + lax.dot_general(
                k_dec, vnb, (((0,), (0,)), ((), ())), preferred_element_type=F32)
            o = o * lax.rsqrt(jnp.mean(o * o, axis=-1, keepdims=True) + NORM_EPS) * normw
            zc = z_ref[pl.ds(r0, CHUNK), vs.start - 2 * GDN_K_WIDTH:vs.stop - 2 * GDN_K_WIDTH].astype(F32)
            y_ref[pl.ds(r0, CHUNK), vs.start - 2 * GDN_K_WIDTH:vs.stop - 2 * GDN_K_WIDTH] = (
                o * _silu(zc)).astype(y_ref.dtype)
        return 0

    lax.fori_loop(0, CHUNKS_PER_BLOCK, chunk_step, 0)


def _gdn_call(qkv, small, z, convw, normw):
    rows = qkv.shape[0]
    n_blocks = rows // ROW_BLOCK
    hist_blocks_per_row_block = ROW_BLOCK // SUBLANES
    return pl.pallas_call(
        _gdn_body,
        grid=(n_blocks,),
        in_specs=[
            pl.BlockSpec((ROW_BLOCK, 3 * GDN_K_WIDTH), lambda i: (i, 0)),
            pl.BlockSpec((SUBLANES, 3 * GDN_K_WIDTH),
                         lambda i: (jnp.maximum(i * hist_blocks_per_row_block - 1, 0), 0)),
            pl.BlockSpec((ROW_BLOCK, LANES), lambda i: (i, 0)),
            pl.BlockSpec((ROW_BLOCK, GDN_V_WIDTH), lambda i: (i, 0)),
            _resident(convw.shape), _resident(normw.shape),
        ],
        out_specs=pl.BlockSpec((ROW_BLOCK, GDN_V_WIDTH), lambda i: (i, 0)),
        out_shape=jax.ShapeDtypeStruct((rows, GDN_V_WIDTH), BF16),
        scratch_shapes=[pltpu.VMEM((GDN_HEADS, GDN_KEY_DIM, GDN_VAL_DIM), F32),
                        pltpu.VMEM((ROW_BLOCK, 3 * GDN_K_WIDTH), F32)],
        compiler_params=pltpu.CompilerParams(dimension_semantics=("arbitrary",),
                                             vmem_limit_bytes=VMEM_LIMIT_BYTES),
        name="gated_delta_rule",
    )(qkv, qkv, small, z, convw, normw)


def _out_body(x_ref, ya_ref, yb_ref, ga_ref, gb_ref, lng_ref, lnb_ref, woa_ref, wob_ref, wo_ref,
              ln1g_ref, ln1b_ref, wgate_ref, wup_ref, wdown_ref, ln2g_ref, ln2b_ref, o_ref):
    h0 = _layer_norm(x_ref[...], lng_ref[...], lnb_ref[...])
    pa = _dot(ya_ref[...], woa_ref[...])
    pb = _dot(yb_ref[...], wob_ref[...])
    mixed = _sigmoid(ga_ref[...].astype(F32)) * pa + _sigmoid(gb_ref[...].astype(F32)) * pb
    h1 = _layer_norm(ALPHA * h0 + _dot(mixed.astype(BF16), wo_ref[...]), ln1g_ref[...], ln1b_ref[...])
    h1b = h1.astype(BF16)
    d_ff = wgate_ref.shape[1]
    half = d_ff // 2
    ffn = jnp.zeros_like(h1)
    for c in range(2):
        sl = slice(c * half, (c + 1) * half)
        gate = _dot(h1b, wgate_ref[:, sl])
        up = _dot(h1b, wup_ref[:, sl])
        ffn = ffn + _dot((_silu(gate) * up).astype(BF16), wdown_ref[sl, :])
    o_ref[...] = _layer_norm(ALPHA * h1 + ffn, ln2g_ref[...], ln2b_ref[...])


def _out_call(x, ya, yb, ga, gb, lng, lnb, woa, wob, wo, ln1g, ln1b, wgate, wup, wdown, ln2g, ln2b):
    seq = x.shape[0]
    n_blocks = seq // ROW_BLOCK
    main = lambda i: (i, 0)
    shifted = lambda i: (i + 1, 0)
    res = [lng, lnb, woa, wob, wo, ln1g, ln1b, wgate, wup, wdown, ln2g, ln2b]
    return pl.pallas_call(
        _out_body,
        grid=(n_blocks,),
        in_specs=[
            pl.BlockSpec((ROW_BLOCK, D_MODEL), main),
            pl.BlockSpec((ROW_BLOCK, ya.shape[1]), main),
            pl.BlockSpec((ROW_BLOCK, GDN_V_WIDTH), shifted),
            pl.BlockSpec((ROW_BLOCK, D_MODEL), shifted),
            pl.BlockSpec((ROW_BLOCK, D_MODEL), shifted),
        ] + [_resident(a.shape) for a in res],
        out_specs=pl.BlockSpec((ROW_BLOCK, D_MODEL), main),
        out_shape=jax.ShapeDtypeStruct((seq, D_MODEL), F32),
        compiler_params=pltpu.CompilerParams(dimension_semantics=("parallel",),
                                             vmem_limit_bytes=VMEM_LIMIT_BYTES),
        name="merge_out_ffn",
    )(x, ya, yb, ga, gb, *res)


def _slot_columns(w, scale=1.0):
    d = w.shape[0]
    w = (w * scale).reshape(d, FOX_HEADS, FOX_HEAD_DIM)
    w = jnp.pad(w, ((0, 0), (0, 0), (0, HEAD_SLOT - FOX_HEAD_DIM)))
    return w.reshape(d, FOX_HEADS * HEAD_SLOT)


def _slot_rows(w):
    d = w.shape[1]
    w = w.reshape(FOX_HEADS, FOX_HEAD_DIM, d)
    w = jnp.pad(w, ((0, 0), (0, HEAD_SLOT - FOX_HEAD_DIM), (0, 0)))
    return w.reshape(FOX_HEADS * HEAD_SLOT, d)


def _constants():
    wide = FOX_HEADS * HEAD_SLOT
    r = np.arange(ROW_BLOCK)
    tri_full = (r[:, None] >= r[None, :])
    tri_chunk = tri_full & ((r[:, None] // CHUNK) == (r[None, :] // CHUNK))
    tri = np.concatenate([tri_full, tri_chunk], axis=0).astype(np.float32)
    e_mat = np.zeros((3 * LANES, wide), np.float32)
    qone = np.zeros((1, wide), np.float32)
    vone = np.zeros((1, wide), np.float32)
    for hd in range(FOX_HEADS):
        for piece in range(3):
            e_mat[piece * LANES + LOGF_LANE0 + hd, hd * HEAD_SLOT + BIAS_LANE + piece] = 1.0
            qone[0, hd * HEAD_SLOT + BIAS_LANE + piece] = 1.0
        vone[0, hd * HEAD_SLOT + BIAS_LANE] = 1.0
    return (jnp.asarray(tri, BF16), jnp.asarray(e_mat, BF16), jnp.asarray(qone), jnp.asarray(vone))


def kernel(x, meta_tokens, ln_in_g, ln_in_b, w_in, b_f, conv_w, a_log, dt_bias, gdn_norm_w,
           w_out_a, w_out_b, w_o, ln1_g, ln1_b, w_gate, w_up, w_down, ln2_g, ln2_b):
    bsz, seq, d = x.shape
    assert bsz == 1 and d == D_MODEL and seq % ROW_BLOCK == 0 and w_in.shape[0] == DEPTH
    x2 = x[0]
    w = w_in[0]
    o = 0
    cols = {}
    for name, width in (("q_a", FOX_WIDTH), ("k_a", FOX_WIDTH), ("v_a", FOX_WIDTH), ("f_a", FOX_HEADS),
                        ("qkv_b", 3 * GDN_K_WIDTH), ("a_b", GDN_HEADS), ("b_b", GDN_HEADS),
                        ("z_b", GDN_V_WIDTH), ("gate_a", D_MODEL), ("gate_b", D_MODEL)):
        cols[name] = w[:, o:o + width]
        o += width
    assert o == w.shape[1]

    wq = _slot_columns(cols["q_a"], FOX_HEAD_DIM ** -0.5).astype(BF16)
    wk = _slot_columns(cols["k_a"]).astype(BF16)
    wv = _slot_columns(cols["v_a"]).astype(BF16)
    wn = jnp.zeros((d, LANES), F32)
    wn = wn.at[:, LOGF_LANE0:LOGF_LANE0 + FOX_HEADS].set(cols["f_a"])
    wn = wn.at[:, G_LANE0:G_LANE0 + GDN_HEADS].set(cols["a_b"])
    wn = wn.at[:, BETA_LANE0:BETA_LANE0 + GDN_HEADS].set(cols["b_b"]).astype(BF16)
    par = jnp.zeros((SUBLANES, LANES), F32)
    par = par.at[0, LOGF_LANE0:LOGF_LANE0 + FOX_HEADS].set(b_f[0])
    par = par.at[1, G_LANE0:G_LANE0 + GDN_HEADS].set(a_log[0])
    par = par.at[2, G_LANE0:G_LANE0 + GDN_HEADS].set(dt_bias[0])
    tri, e_mat, qone, vone = _constants()
    prefix = jnp.zeros((PRE, d), F32).at[LEAD:].set(meta_tokens.astype(F32))
    row = lambda v: v.reshape(1, -1).astype(F32)

    qa, ka, va, qkv, z, ga, gb, small, r_blk = _proj_call(
        prefix, x2, row(ln_in_g), row(ln_in_b), wq, wk, wv, cols["qkv_b"].astype(BF16),
        cols["z_b"].astype(BF16), cols["gate_a"].astype(BF16), cols["gate_b"].astype(BF16),
        wn, par, tri, e_mat, qone, vone)

    r_flat = r_blk[:, :SUBLANES].reshape(-1, SUBLANES, SUBLANES)[:, 0, :].reshape(-1)
    ya = _attn_call(r_flat, qa, ka, va)
    yb = _gdn_call(qkv, small, z, conv_w[0].astype(F32), row(gdn_norm_w[0]))
    out = _out_call(x2, ya, yb, ga, gb, row(ln_in_g), row(ln_in_b),
                    _slot_rows(w_out_a[0]).astype(BF16), w_out_b[0].astype(BF16), w_o[0].astype(BF16),
                    row(ln1_g[0]), row(ln1_b[0]), w_gate[0].astype(BF16), w_up[0].astype(BF16),
                    w_down[0].astype(BF16), row(ln2_g[0]), row(ln2_b[0]))
    return out[None]
```

```python
import functools
import math

import jax
import jax.numpy as jnp
import numpy as np
from jax import lax
from jax.experimental import pallas as pl
from jax.experimental.pallas import tpu as pltpu

D_MODEL = 1024
CHUNK = 64
N_META = 16
FOX_HEADS = 8
FOX_HEAD_DIM = 64
FOX_WIDTH = FOX_HEADS * FOX_HEAD_DIM
GDN_HEADS = 8
GDN_KEY_DIM = 128
GDN_VAL_DIM = 128
GDN_K_WIDTH = GDN_HEADS * GDN_KEY_DIM
GDN_V_WIDTH = GDN_HEADS * GDN_VAL_DIM
CONV_WIDTH = 4
DEPTH = 1
ALPHA = (2.0 * DEPTH) ** 0.25
LN_EPS = 1e-5
NORM_EPS = 1e-6

LANES = 128
SUBLANES = 8
VMEM_LIMIT_BYTES = 56 * 1024 * 1024

ROW_BLOCK = 512
PRE = ROW_BLOCK
LEAD = PRE - N_META
CHUNKS_PER_BLOCK = ROW_BLOCK // CHUNK
SUPER = 256
HEAD_GROUP = 4
HEAD_SLOT = LANES
BIAS_LANE = FOX_HEAD_DIM
MASK_BIAS = -1e30
NEG_INIT = -1e30
LOG2E = math.log2(math.e)
F32 = jnp.float32
BF16 = jnp.bfloat16

LOGF_LANE0 = 0
G_LANE0 = 8
BETA_LANE0 = 16


def _nt_dot(a, b):
    return lax.dot_general(a, b, (((1,), (1,)), ((), ())), preferred_element_type=F32)


def _dot(a, b):
    return jnp.dot(a, b, preferred_element_type=F32)


def _layer_norm(x, g, b):
    mu = jnp.mean(x, axis=-1, keepdims=True)
    xc = x - mu
    var = jnp.mean(xc * xc, axis=-1, keepdims=True)
    return xc * lax.rsqrt(var + LN_EPS) * g + b


def _softplus(x):
    return jnp.maximum(x, 0.0) + jnp.log(1.0 + jnp.exp(-jnp.abs(x)))


def _sigmoid(x):
    return 1.0 / (1.0 + jnp.exp(-x))


def _silu(x):
    return x * _sigmoid(x)


def _split3(x):
    hi = x.astype(BF16)
    r1 = x - hi.astype(F32)
    mid = r1.astype(BF16)
    lo = (r1 - mid.astype(F32)).astype(BF16)
    return hi, mid, lo


def _resident(shape):
    nd = len(shape)
    return pl.BlockSpec(shape, lambda *_: (0,) * nd, pipeline_mode=pl.Buffered(1))


def _proj_body(prefix_ref, x_ref, lng_ref, lnb_ref, wq_ref, wk_ref, wv_ref, wg_ref, wz_ref, wga_ref,
               wgb_ref, wn_ref, par_ref, tri_ref, e_ref, qone_ref, vone_ref,
               qa_ref, ka_ref, va_ref, qkv_ref, z_ref, ga_ref, gb_ref, small_ref, r_ref,
               carry_ref):
    i = pl.program_id(0)

    @pl.when(i == 0)
    def _():
        carry_ref[...] = jnp.zeros_like(carry_ref)

    src = jnp.where(i == 0, prefix_ref[...], x_ref[...])
    h = _layer_norm(src, lng_ref[...], lnb_ref[...])
    row = lax.broadcasted_iota(jnp.int32, (ROW_BLOCK, 1), 0)
    valid = jnp.logical_or(i > 0, row >= LEAD)
    hb = jnp.where(valid, h, 0.0).astype(BF16)

    nar = _dot(hb, wn_ref[...])
    lane = lax.broadcasted_iota(jnp.int32, (1, LANES), 1)
    b_f = par_ref[0:1, :]
    a_log = par_ref[1:2, :]
    dt_bias = par_ref[2:3, :]
    xf = nar + b_f
    logf = jnp.minimum(xf, 0.0) - jnp.log(1.0 + jnp.exp(-jnp.abs(xf)))
    g = -jnp.exp(a_log) * _softplus(nar + dt_bias)
    beta = _sigmoid(nar)
    is_logf = lane < G_LANE0
    is_g = jnp.logical_and(lane >= G_LANE0, lane < BETA_LANE0)
    vals = jnp.where(is_logf, logf, jnp.where(is_g, g, 0.0))

    hi, mid, lo = _split3(vals)
    tri = tri_ref[...]
    cs = _dot(tri, hi) + _dot(tri, mid) + _dot(tri, lo)
    d_blk = cs[:ROW_BLOCK]
    gc = cs[ROW_BLOCK:]
    carry = carry_ref[...]
    r_ref[...] = jnp.broadcast_to(carry, r_ref.shape)
    carry_ref[...] = carry + d_blk[ROW_BLOCK - 1:ROW_BLOCK, :]
    small_ref[...] = jnp.where(is_g, gc, jnp.where(lane >= BETA_LANE0, beta, d_blk))

    nd = jnp.where(valid, -LOG2E * d_blk, MASK_BIAS)
    nd = jnp.where(is_logf, nd, 0.0)
    phi, pmid, plo = _split3(nd)
    pieces = jnp.concatenate([phi, pmid, plo], axis=1)
    qa_ref[...] = (_dot(hb, wq_ref[...]) + qone_ref[...]).astype(BF16)
    ka_ref[...] = (_dot(hb, wk_ref[...]) + _dot(pieces, e_ref[...])).astype(BF16)
    va_ref[...] = (_dot(hb, wv_ref[...]) + vone_ref[...]).astype(BF16)

    for c in range(3):
        sl = slice(c * GDN_K_WIDTH, (c + 1) * GDN_K_WIDTH)
        qkv_ref[:, sl] = _dot(hb, wg_ref[:, sl]).astype(BF16)
    z_ref[...] = _dot(hb, wz_ref[...]).astype(BF16)
    ga_ref[...] = _dot(hb, wga_ref[...]).astype(BF16)
    gb_ref[...] = _dot(hb, wgb_ref[...]).astype(BF16)


def _proj_call(prefix, x, lng, lnb, wq, wk, wv, wg, wz, wga, wgb, wn, par, tri, e_mat, qone, vone):
    seq = x.shape[0]
    n_blocks = seq // ROW_BLOCK + 1
    rows = n_blocks * ROW_BLOCK
    wide = FOX_HEADS * HEAD_SLOT

    def row_spec(width):
        return pl.BlockSpec((ROW_BLOCK, width), lambda i: (i, 0))

    in_specs = [
        _resident(prefix.shape),
        pl.BlockSpec((ROW_BLOCK, D_MODEL), lambda i: (jnp.maximum(i - 1, 0), 0)),
        _resident(lng.shape), _resident(lnb.shape),
        _resident(wq.shape), _resident(wk.shape), _resident(wv.shape), _resident(wg.shape),
        _resident(wz.shape), _resident(wga.shape), _resident(wgb.shape), _resident(wn.shape),
        _resident(par.shape), _resident(tri.shape), _resident(e_mat.shape),
        _resident(qone.shape), _resident(vone.shape),
    ]
    out_shape = [
        jax.ShapeDtypeStruct((rows, wide), BF16),
        jax.ShapeDtypeStruct((rows, wide), BF16),
        jax.ShapeDtypeStruct((rows, wide), BF16),
        jax.ShapeDtypeStruct((rows, 3 * GDN_K_WIDTH), BF16),
        jax.ShapeDtypeStruct((rows, GDN_V_WIDTH), BF16),
        jax.ShapeDtypeStruct((rows, D_MODEL), BF16),
        jax.ShapeDtypeStruct((rows, D_MODEL), BF16),
        jax.ShapeDtypeStruct((rows, LANES), F32),
        jax.ShapeDtypeStruct((n_blocks * SUBLANES, LANES), F32),
    ]
    out_specs = [row_spec(wide), row_spec(wide), row_spec(wide), row_spec(3 * GDN_K_WIDTH),
                 row_spec(GDN_V_WIDTH), row_spec(D_MODEL), row_spec(D_MODEL), row_spec(LANES),
                 pl.BlockSpec((SUBLANES, LANES), lambda i: (i, 0))]
    return pl.pallas_call(
        _proj_body,
        grid=(n_blocks,),
        in_specs=in_specs,
        out_specs=out_specs,
        out_shape=out_shape,
        scratch_shapes=[pltpu.VMEM((1, LANES), F32)],
        compiler_params=pltpu.CompilerParams(dimension_semantics=("arbitrary",),
                                             vmem_limit_bytes=VMEM_LIMIT_BYTES),
        name="ln_in_proj",
    )(prefix, x, lng, lnb, wq, wk, wv, wg, wz, wga, wgb, wn, par, tri, e_mat, qone, vone)


def _attn_body(r_ref, q_ref, k_ref, v_ref, o_ref, s_scr, m_scr, acc_scr):
    h = pl.program_id(0)
    i = pl.program_id(1)
    q = q_ref[...]

    def scores(j):
        start = pl.multiple_of(j * ROW_BLOCK, ROW_BLOCK)
        return _nt_dot(q, k_ref[pl.ds(start, ROW_BLOCK), :])

    def consume(j, s):
        start = pl.multiple_of(j * ROW_BLOCK, ROW_BLOCK)
        vb = v_ref[pl.ds(start, ROW_BLOCK), :]
        rj = r_ref[j * SUBLANES + h] * LOG2E
        m = m_scr[...]
        m_new = jnp.maximum(m, jnp.max(s, axis=1, keepdims=True) - rj)
        p = jnp.exp2(s - (m_new + rj))
        alpha = jnp.exp2(m - m_new)
        acc_scr[...] = alpha * acc_scr[...] + _dot(p.astype(BF16), vb)
        m_scr[...] = m_new

    m_scr[...] = jnp.full(m_scr.shape, NEG_INIT, F32)
    acc_scr[...] = jnp.zeros(acc_scr.shape, F32)
    s_scr[0] = scores(0)

    def pair(t, _):
        j = 2 * t
        s_scr[1] = scores(j + 1)
        consume(j, s_scr[0])
        s_scr[0] = scores(j + 2)
        consume(j + 1, s_scr[1])
        return 0

    n_full = i + 1
    lax.fori_loop(0, n_full // 2, pair, 0)
    last = i + 1
    rr = lax.broadcasted_iota(jnp.int32, (ROW_BLOCK, ROW_BLOCK), 0)
    cc = lax.broadcasted_iota(jnp.int32, (ROW_BLOCK, ROW_BLOCK), 1)

    @pl.when(lax.rem(n_full, 2) == 1)
    def _():
        s_scr[1] = scores(last)
        consume(last - 1, s_scr[0])
        consume(last, jnp.where(rr >= cc, s_scr[1], MASK_BIAS))

    @pl.when(lax.rem(n_full, 2) == 0)
    def _():
        consume(last, jnp.where(rr >= cc, s_scr[0], MASK_BIAS))

    acc = acc_scr[...]
    denom = acc[:, FOX_HEAD_DIM:FOX_HEAD_DIM + 1]
    o_ref[...] = (acc / denom).astype(o_ref.dtype)


def _attn_call(r_flat, qa, ka, va):
    rows = qa.shape[0]
    n_q = rows // ROW_BLOCK - 1
    whole = lambda h, i: (0, h)
    return pl.pallas_call(
        _attn_body,
        grid=(FOX_HEADS, n_q),
        in_specs=[
            pl.BlockSpec(memory_space=pltpu.SMEM),
            pl.BlockSpec((ROW_BLOCK, HEAD_SLOT), lambda h, i: (i + 1, h)),
            pl.BlockSpec((rows, HEAD_SLOT), whole),
            pl.BlockSpec((rows, HEAD_SLOT), whole),
        ],
        out_specs=pl.BlockSpec((ROW_BLOCK, HEAD_SLOT), lambda h, i: (i, h)),
        out_shape=jax.ShapeDtypeStruct((n_q * ROW_BLOCK, FOX_HEADS * HEAD_SLOT), BF16),
        scratch_shapes=[pltpu.VMEM((2, ROW_BLOCK, ROW_BLOCK), F32),
                        pltpu.VMEM((ROW_BLOCK, 1), F32),
                        pltpu.VMEM((ROW_BLOCK, HEAD_SLOT), F32)],
        compiler_params=pltpu.CompilerParams(dimension_semantics=("arbitrary", "arbitrary"),
                                             vmem_limit_bytes=VMEM_LIMIT_BYTES),
        name="fox_attention",
    )(r_flat, qa, ka, va)


def _gdn_body(qkv_ref, hist_ref, small_ref, z_ref, convw_ref, normw_ref, y_ref,
              state_ref, xn_ref, u_scr, w_scr, attn_scr, qd_scr, kd_scr, vnew_scr):
    i = pl.program_id(0)

    @pl.when(i == 0)
    def _():
        state_ref[...] = jnp.zeros_like(state_ref)
        vnew_scr[...] = jnp.zeros_like(vnew_scr)

    hist = jnp.where(i > 0, hist_ref[...].astype(F32), 0.0)
    cur = qkv_ref[...].astype(F32)
    ext = jnp.concatenate([hist, cur], axis=0)
    conv = jnp.zeros_like(cur)
    for t in range(CONV_WIDTH):
        off = SUBLANES - (CONV_WIDTH - 1) + t
        conv = conv + convw_ref[t:t + 1, :] * ext[off:off + ROW_BLOCK, :]
    act = _silu(conv)
    for hd in range(2 * GDN_HEADS):
        sl = slice(hd * GDN_KEY_DIM, (hd + 1) * GDN_KEY_DIM)
        blk = act[:, sl]
        scale = lax.rsqrt(jnp.sum(blk * blk, axis=-1, keepdims=True) + NORM_EPS)
        if hd < GDN_HEADS:
            scale = scale * (GDN_KEY_DIM ** -0.5)
        xn_ref[:, sl] = blk * scale
    xn_ref[:, 2 * GDN_K_WIDTH:] = act[:, 2 * GDN_K_WIDTH:]

    ri = lax.broadcasted_iota(jnp.int32, (SUPER, SUPER), 0)
    ci = lax.broadcasted_iota(jnp.int32, (SUPER, SUPER), 1)
    same = (ri // CHUNK) == (ci // CHUNK)
    incl = jnp.logical_and(same, ri >= ci)
    strict = jnp.logical_and(same, ri > ci)
    eye = (ri == ci).astype(F32)

    def head_slices(hd):
        qs = slice(hd * GDN_KEY_DIM, (hd + 1) * GDN_KEY_DIM)
        ks = slice(GDN_K_WIDTH + hd * GDN_KEY_DIM, GDN_K_WIDTH + (hd + 1) * GDN_KEY_DIM)
        vs = slice(2 * GDN_K_WIDTH + hd * GDN_VAL_DIM, 2 * GDN_K_WIDTH + (hd + 1) * GDN_VAL_DIM)
        return qs, ks, vs

    for sc in range(ROW_BLOCK // SUPER):
        rows = slice(sc * SUPER, (sc + 1) * SUPER)
        sm = small_ref[rows, :]
        sm_t = sm.T
        for h0 in range(0, GDN_HEADS, HEAD_GROUP):
            heads = range(h0, h0 + HEAD_GROUP)
            pw, t_inv, keep = {}, {}, {}
            for hd in heads:
                qs, ks, vs = head_slices(hd)
                qc = xn_ref[rows, qs]
                kc = xn_ref[rows, ks]
                vc = xn_ref[rows, vs]
                gcol = sm[:, G_LANE0 + hd:G_LANE0 + hd + 1]
                bcol = sm[:, BETA_LANE0 + hd:BETA_LANE0 + hd + 1]
                grow = sm_t[G_LANE0 + hd:G_LANE0 + hd + 1, :]
                glast = jnp.concatenate(
                    [jnp.broadcast_to(gcol[(c + 1) * CHUNK - 1:(c + 1) * CHUNK, :], (CHUNK, 1))
                     for c in range(SUPER // CHUNK)], axis=0)
                decay = jnp.exp(jnp.where(incl, gcol - grow, -jnp.inf))
                egc = jnp.exp(gcol)
                kb = kc * bcol
                kcb = kc.astype(BF16)
                a_mat = jnp.where(strict, _nt_dot(kb.astype(BF16), kcb) * decay, 0.0)
                pw[hd] = -a_mat
                t_inv[hd] = eye + pw[hd]
                attn_scr[hd, rows, :] = jnp.where(
                    incl, _nt_dot(qc.astype(BF16), kcb) * decay, 0.0).astype(BF16)
                qd_scr[rows, qs] = (qc * egc).astype(BF16)
                kd_scr[rows, qs] = (kc * jnp.exp(glast - gcol)).astype(BF16)
                keep[hd] = jnp.concatenate([vc * bcol, kb * egc], axis=1).astype(BF16)
            for _k in range(int(math.log2(CHUNK)) - 1):
                for hd in heads:
                    pwb = pw[hd].astype(BF16)
                    pw[hd] = _dot(pwb, pwb)
                for hd in heads:
                    t_inv[hd] = t_inv[hd] + _dot(t_inv[hd].astype(BF16), pw[hd].astype(BF16))
            for hd in heads:
                qs, _, _ = head_slices(hd)
                uw = _dot(t_inv[hd].astype(BF16), keep[hd])
                u_scr[rows, qs] = uw[:, :GDN_VAL_DIM]
                w_scr[rows, qs] = uw[:, GDN_VAL_DIM:].astype(BF16)

    normw = normw_ref[...]

    def chunk_step(c, _):
        r0 = pl.multiple_of(c * CHUNK, CHUNK)
        in_sup = pl.multiple_of(lax.rem(c, SUPER // CHUNK) * CHUNK, CHUNK)
        g_tail = small_ref[pl.ds(r0 + CHUNK - SUBLANES, SUBLANES), :]
        g_end = g_tail[SUBLANES - 1:SUBLANES, :]
        states, sbs, vnbs, outs = {}, {}, {}, {}
        for hd in range(GDN_HEADS):
            qs, _, _ = head_slices(hd)
            states[hd] = state_ref[hd]
            sbs[hd] = states[hd].astype(BF16)
            v_new = u_scr[pl.ds(r0, CHUNK), qs] - _dot(w_scr[pl.ds(r0, CHUNK), qs], sbs[hd])
            vnbs[hd] = v_new.astype(BF16)
            vnew_scr[hd, pl.ds(in_sup, CHUNK), :] = vnbs[hd]
        for hd in range(GDN_HEADS):
            qs, _, _ = head_slices(hd)
            outs[hd] = (_dot(qd_scr[pl.ds(r0, CHUNK), qs], sbs[hd])
                        + _dot(attn_scr[hd, pl.ds(r0, CHUNK), :], vnew_scr[hd]))
        for hd in range(GDN_HEADS):
            qs, _, _ = head_slices(hd)
            gl = jnp.exp(g_end[:, G_LANE0 + hd:G_LANE0 + hd + 1])
            state_ref[hd] = states[hd] * gl + lax.dot_general(
                kd_scr[pl.ds(r0, CHUNK), qs], vnbs[hd], (((0,), (0,)), ((), ())),
                preferred_element_type=F32)
        for hd in range(GDN_HEADS):
            qs, _, _ = head_slices(hd)
            o = outs[hd]
            o = o * lax.rsqrt(jnp.mean(o * o, axis=-1, keepdims=True) + NORM_EPS) * normw
            zc = z_ref[pl.ds(r0, CHUNK), qs].astype(F32)
            y_ref[pl.ds(r0, CHUNK), qs] = (o * _silu(zc)).astype(y_ref.dtype)
        return 0

    lax.fori_loop(0, CHUNKS_PER_BLOCK, chunk_step, 0)


def _gdn_call(qkv, small, z, convw, normw):
    rows = qkv.shape[0]
    n_blocks = rows // ROW_BLOCK
    hist_blocks_per_row_block = ROW_BLOCK // SUBLANES
    return pl.pallas_call(
        _gdn_body,
        grid=(n_blocks,),
        in_specs=[
            pl.BlockSpec((ROW_BLOCK, 3 * GDN_K_WIDTH), lambda i: (i, 0)),
            pl.BlockSpec((SUBLANES, 3 * GDN_K_WIDTH),
                         lambda i: (jnp.maximum(i * hist_blocks_per_row_block - 1, 0), 0)),
            pl.BlockSpec((ROW_BLOCK, LANES), lambda i: (i, 0)),
            pl.BlockSpec((ROW_BLOCK, GDN_V_WIDTH), lambda i: (i, 0)),
            _resident(convw.shape), _resident(normw.shape),
        ],
        out_specs=pl.BlockSpec((ROW_BLOCK, GDN_V_WIDTH), lambda i: (i, 0)),
        out_shape=jax.ShapeDtypeStruct((rows, GDN_V_WIDTH), BF16),
        scratch_shapes=[pltpu.VMEM((GDN_HEADS, GDN_KEY_DIM, GDN_VAL_DIM), F32),
                        pltpu.VMEM((ROW_BLOCK, 3 * GDN_K_WIDTH), F32),
                        pltpu.VMEM((ROW_BLOCK, GDN_V_WIDTH), F32),
                        pltpu.VMEM((ROW_BLOCK, GDN_K_WIDTH), BF16),
                        pltpu.VMEM((GDN_HEADS, ROW_BLOCK, SUPER), BF16),
                        pltpu.VMEM((ROW_BLOCK, GDN_K_WIDTH), BF16),
                        pltpu.VMEM((ROW_BLOCK, GDN_K_WIDTH), BF16),
                        pltpu.VMEM((GDN_HEADS, SUPER, GDN_VAL_DIM), BF16)],
        compiler_params=pltpu.CompilerParams(dimension_semantics=("arbitrary",),
                                             vmem_limit_bytes=VMEM_LIMIT_BYTES),
        name="gated_delta_rule",
    )(qkv, qkv, small, z, convw, normw)


def _out_body(x_ref, ya_ref, yb_ref, ga_ref, gb_ref, lng_ref, lnb_ref, woa_ref, wob_ref, wo_ref,
              ln1g_ref, ln1b_ref, wgate_ref, wup_ref, wdown_ref, ln2g_ref, ln2b_ref, o_ref):
    h0 = _layer_norm(x_ref[...], lng_ref[...], lnb_ref[...])
    pa = _dot(ya_ref[...], woa_ref[...])
    pb = _dot(yb_ref[...], wob_ref[...])
    mixed = _sigmoid(ga_ref[...].astype(F32)) * pa + _sigmoid(gb_ref[...].astype(F32)) * pb
    h1 = _layer_norm(ALPHA * h0 + _dot(mixed.astype(BF16), wo_ref[...]), ln1g_ref[...], ln1b_ref[...])
    h1b = h1.astype(BF16)
    d_ff = wgate_ref.shape[1]
    half = d_ff // 2
    ffn = jnp.zeros_like(h1)
    for c in range(2):
        sl = slice(c * half, (c + 1) * half)
        gate = _dot(h1b, wgate_ref[:, sl])
        up = _dot(h1b, wup_ref[:, sl])
        ffn = ffn + _dot((_silu(gate) * up).astype(BF16), wdown_ref[sl, :])
    o_ref[...] = _layer_norm(ALPHA * h1 + ffn, ln2g_ref[...], ln2b_ref[...])


def _out_call(x, ya, yb, ga, gb, lng, lnb, woa, wob, wo, ln1g, ln1b, wgate, wup, wdown, ln2g, ln2b):
    seq = x.shape[0]
    n_blocks = seq // ROW_BLOCK
    main = lambda i: (i, 0)
    shifted = lambda i: (i + 1, 0)
    res = [lng, lnb, woa, wob, wo, ln1g, ln1b, wgate, wup, wdown, ln2g, ln2b]
    return pl.pallas_call(
        _out_body,
        grid=(n_blocks,),
        in_specs=[
            pl.BlockSpec((ROW_BLOCK, D_MODEL), main),
            pl.BlockSpec((ROW_BLOCK, ya.shape[1]), main),
            pl.BlockSpec((ROW_BLOCK, GDN_V_WIDTH), shifted),
            pl.BlockSpec((ROW_BLOCK, D_MODEL), shifted),
            pl.BlockSpec((ROW_BLOCK, D_MODEL), shifted),
        ] + [_resident(a.shape) for a in res],
        out_specs=pl.BlockSpec((ROW_BLOCK, D_MODEL), main),
        out_shape=jax.ShapeDtypeStruct((seq, D_MODEL), F32),
        compiler_params=pltpu.CompilerParams(dimension_semantics=("parallel",),
                                             vmem_limit_bytes=VMEM_LIMIT_BYTES),
        name="merge_out_ffn",
    )(x, ya, yb, ga, gb, *res)


def _slot_columns(w, scale=1.0):
    d = w.shape[0]
    w = (w * scale).reshape(d, FOX_HEADS, FOX_HEAD_DIM)
    w = jnp.pad(w, ((0, 0), (0, 0), (0, HEAD_SLOT - FOX_HEAD_DIM)))
    return w.reshape(d, FOX_HEADS * HEAD_SLOT)


def _slot_rows(w):
    d = w.shape[1]
    w = w.reshape(FOX_HEADS, FOX_HEAD_DIM, d)
    w = jnp.pad(w, ((0, 0), (0, HEAD_SLOT - FOX_HEAD_DIM), (0, 0)))
    return w.reshape(FOX_HEADS * HEAD_SLOT, d)


def _constants():
    wide = FOX_HEADS * HEAD_SLOT
    r = np.arange(ROW_BLOCK)
    tri_full = (r[:, None] >= r[None, :])
    tri_chunk = tri_full & ((r[:, None] // CHUNK) == (r[None, :] // CHUNK))
    tri = np.concatenate([tri_full, tri_chunk], axis=0).astype(np.float32)
    e_mat = np.zeros((3 * LANES, wide), np.float32)
    qone = np.zeros((1, wide), np.float32)
    vone = np.zeros((1, wide), np.float32)
    for hd in range(FOX_HEADS):
        for piece in range(3):
            e_mat[piece * LANES + LOGF_LANE0 + hd, hd * HEAD_SLOT + BIAS_LANE + piece] = 1.0
            qone[0, hd * HEAD_SLOT + BIAS_LANE + piece] = 1.0
        vone[0, hd * HEAD_SLOT + BIAS_LANE] = 1.0
    return (jnp.asarray(tri, BF16), jnp.asarray(e_mat, BF16), jnp.asarray(qone), jnp.asarray(vone))


def kernel(x, meta_tokens, ln_in_g, ln_in_b, w_in, b_f, conv_w, a_log, dt_bias, gdn_norm_w,
           w_out_a, w_out_b, w_o, ln1_g, ln1_b, w_gate, w_up, w_down, ln2_g, ln2_b):
    bsz, seq, d = x.shape
    assert bsz == 1 and d == D_MODEL and seq % ROW_BLOCK == 0 and w_in.shape[0] == DEPTH
    x2 = x[0]
    w = w_in[0]
    o = 0
    cols = {}
    for name, width in (("q_a", FOX_WIDTH), ("k_a", FOX_WIDTH), ("v_a", FOX_WIDTH), ("f_a", FOX_HEADS),
                        ("qkv_b", 3 * GDN_K_WIDTH), ("a_b", GDN_HEADS), ("b_b", GDN_HEADS),
                        ("z_b", GDN_V_WIDTH), ("gate_a", D_MODEL), ("gate_b", D_MODEL)):
        cols[name] = w[:, o:o + width]
        o += width
    assert o == w.shape[1]

    wq = _slot_columns(cols["q_a"], FOX_HEAD_DIM ** -0.5 * LOG2E).astype(BF16)
    wk = _slot_columns(cols["k_a"]).astype(BF16)
    wv = _slot_columns(cols["v_a"]).astype(BF16)
    wn = jnp.zeros((d, LANES), F32)
    wn = wn.at[:, LOGF_LANE0:LOGF_LANE0 + FOX_HEADS].set(cols["f_a"])
    wn = wn.at[:, G_LANE0:G_LANE0 + GDN_HEADS].set(cols["a_b"])
    wn = wn.at[:, BETA_LANE0:BETA_LANE0 + GDN_HEADS].set(cols["b_b"]).astype(BF16)
    par = jnp.zeros((SUBLANES, LANES), F32)
    par = par.at[0, LOGF_LANE0:LOGF_LANE0 + FOX_HEADS].set(b_f[0])
    par = par.at[1, G_LANE0:G_LANE0 + GDN_HEADS].set(a_log[0])
    par = par.at[2, G_LANE0:G_LANE0 + GDN_HEADS].set(dt_bias[0])
    tri, e_mat, qone, vone = _constants()
    prefix = jnp.zeros((PRE, d), F32).at[LEAD:].set(meta_tokens.astype(F32))
    row = lambda v: v.reshape(1, -1).astype(F32)

    qa, ka, va, qkv, z, ga, gb, small, r_blk = _proj_call(
        prefix, x2, row(ln_in_g), row(ln_in_b), wq, wk, wv, cols["qkv_b"].astype(BF16),
        cols["z_b"].astype(BF16), cols["gate_a"].astype(BF16), cols["gate_b"].astype(BF16),
        wn, par, tri, e_mat, qone, vone)

    r_flat = r_blk[:, :SUBLANES].reshape(-1, SUBLANES, SUBLANES)[:, 0, :].reshape(-1)
    ya = _attn_call(r_flat, qa, ka, va)
    yb = _gdn_call(qkv, small, z, conv_w[0].astype(F32), row(gdn_norm_w[0]))
    out = _out_call(x2, ya, yb, ga, gb, row(ln_in_g), row(ln_in_b),
                    _slot_rows(w_out_a[0]).astype(BF16), w_out_b[0].astype(BF16), w_o[0].astype(BF16),
                    row(ln1_g[0]), row(ln1_b[0]), w_gate[0].astype(BF16), w_up[0].astype(BF16),
                    w_down[0].astype(BF16), row(ln2_g[0]), row(ln2_b[0]))
    return out[None]
```

```python
import functools
import math

import jax
import jax.numpy as jnp
import numpy as np
from jax import lax
from jax.experimental import pallas as pl
from jax.experimental.pallas import tpu as pltpu

D_MODEL = 1024
CHUNK = 64
N_META = 16
FOX_HEADS = 8
FOX_HEAD_DIM = 64
FOX_WIDTH = FOX_HEADS * FOX_HEAD_DIM
GDN_HEADS = 8
GDN_KEY_DIM = 128
GDN_VAL_DIM = 128
GDN_K_WIDTH = GDN_HEADS * GDN_KEY_DIM
GDN_V_WIDTH = GDN_HEADS * GDN_VAL_DIM
CONV_WIDTH = 4
DEPTH = 1
ALPHA = (2.0 * DEPTH) ** 0.25
LN_EPS = 1e-5
NORM_EPS = 1e-6

LANES = 128
SUBLANES = 8
VMEM_LIMIT_BYTES = 56 * 1024 * 1024

ROW_BLOCK = 512
PRE = ROW_BLOCK
LEAD = PRE - N_META
CHUNKS_PER_BLOCK = ROW_BLOCK // CHUNK
Q_BLOCK = 2 * ROW_BLOCK
SUPER = 256
HEAD_GROUP = 4
HEAD_SLOT = LANES
BIAS_LANE = FOX_HEAD_DIM
MASK_BIAS = -1e30
NEG_INIT = -1e30
LOG2E = math.log2(math.e)
F32 = jnp.float32
BF16 = jnp.bfloat16

LOGF_LANE0 = 0
G_LANE0 = 8
BETA_LANE0 = 16


def _nt_dot(a, b):
    return lax.dot_general(a, b, (((1,), (1,)), ((), ())), preferred_element_type=F32)


def _dot(a, b):
    return jnp.dot(a, b, preferred_element_type=F32)


def _layer_norm(x, g, b):
    mu = jnp.mean(x, axis=-1, keepdims=True)
    xc = x - mu
    var = jnp.mean(xc * xc, axis=-1, keepdims=True)
    return xc * lax.rsqrt(var + LN_EPS) * g + b


def _softplus(x):
    return jnp.maximum(x, 0.0) + jnp.log(1.0 + jnp.exp(-jnp.abs(x)))


def _sigmoid(x):
    return 1.0 / (1.0 + jnp.exp(-x))


def _silu(x):
    return x * _sigmoid(x)


def _split3(x):
    hi = x.astype(BF16)
    r1 = x - hi.astype(F32)
    mid = r1.astype(BF16)
    lo = (r1 - mid.astype(F32)).astype(BF16)
    return hi, mid, lo


def _resident(shape):
    nd = len(shape)
    return pl.BlockSpec(shape, lambda *_: (0,) * nd, pipeline_mode=pl.Buffered(1))


def _proj_body(prefix_ref, x_ref, lng_ref, lnb_ref, wq_ref, wk_ref, wv_ref, wg_ref, wz_ref, wga_ref,
               wgb_ref, wn_ref, par_ref, tri_ref, e_ref,
               qt_ref, ka_ref, vt_ref, qkv_ref, z_ref, ga_ref, gb_ref, small_ref, r_ref,
               carry_ref):
    i = pl.program_id(0)

    @pl.when(i == 0)
    def _():
        carry_ref[...] = jnp.zeros_like(carry_ref)

    src = jnp.where(i == 0, prefix_ref[...], x_ref[...])
    h = _layer_norm(src, lng_ref[...], lnb_ref[...])
    row = lax.broadcasted_iota(jnp.int32, (ROW_BLOCK, 1), 0)
    valid = jnp.logical_or(i > 0, row >= LEAD)
    hb = jnp.where(valid, h, 0.0).astype(BF16)

    nar = _dot(hb, wn_ref[...])
    lane = lax.broadcasted_iota(jnp.int32, (1, LANES), 1)
    b_f = par_ref[0:1, :]
    a_log = par_ref[1:2, :]
    dt_bias = par_ref[2:3, :]
    xf = nar + b_f
    logf = jnp.minimum(xf, 0.0) - jnp.log(1.0 + jnp.exp(-jnp.abs(xf)))
    g = -jnp.exp(a_log) * _softplus(nar + dt_bias)
    beta = _sigmoid(nar)
    is_logf = lane < G_LANE0
    is_g = jnp.logical_and(lane >= G_LANE0, lane < BETA_LANE0)
    vals = jnp.where(is_logf, logf, jnp.where(is_g, g, 0.0))

    hi, mid, lo = _split3(vals)
    tri = tri_ref[...]
    cs = _dot(tri, hi) + _dot(tri, mid) + _dot(tri, lo)
    d_blk = cs[:ROW_BLOCK]
    gc = cs[ROW_BLOCK:]
    carry = carry_ref[...]
    r_ref[...] = jnp.broadcast_to(carry, r_ref.shape)
    carry_ref[...] = carry + d_blk[ROW_BLOCK - 1:ROW_BLOCK, :]
    small_ref[...] = jnp.where(is_g, gc, jnp.where(lane >= BETA_LANE0, beta, d_blk))

    nd = jnp.where(valid, -LOG2E * d_blk, MASK_BIAS)
    nd = jnp.where(is_logf, nd, 0.0)
    phi, pmid, plo = _split3(nd)
    pieces = jnp.concatenate([phi, pmid, plo], axis=1)
    slot_row = lax.broadcasted_iota(jnp.int32, (FOX_HEADS * HEAD_SLOT, ROW_BLOCK), 0) & (HEAD_SLOT - 1)
    is_bias_row = jnp.logical_and(slot_row >= BIAS_LANE, slot_row < BIAS_LANE + 3)
    qt_ref[0] = jnp.where(is_bias_row, 1.0, _nt_dot(wq_ref[...], hb)).astype(BF16)
    ka_ref[...] = (_dot(hb, wk_ref[...]) + _dot(pieces, e_ref[...])).astype(BF16)
    vt_ref[0] = jnp.where(slot_row == BIAS_LANE, 1.0, _nt_dot(wv_ref[...], hb)).astype(BF16)

    for c in range(3):
        sl = slice(c * GDN_K_WIDTH, (c + 1) * GDN_K_WIDTH)
        qkv_ref[:, sl] = _dot(hb, wg_ref[:, sl]).astype(BF16)
    z_ref[...] = _dot(hb, wz_ref[...]).astype(BF16)
    ga_ref[...] = _dot(hb, wga_ref[...]).astype(BF16)
    gb_ref[...] = _dot(hb, wgb_ref[...]).astype(BF16)


def _proj_call(prefix, x, lng, lnb, wq, wk, wv, wg, wz, wga, wgb, wn, par, tri, e_mat):
    seq = x.shape[0]
    n_blocks = seq // ROW_BLOCK + 1
    rows = n_blocks * ROW_BLOCK
    wide = FOX_HEADS * HEAD_SLOT

    def row_spec(width):
        return pl.BlockSpec((ROW_BLOCK, width), lambda i: (i, 0))

    in_specs = [
        _resident(prefix.shape),
        pl.BlockSpec((ROW_BLOCK, D_MODEL), lambda i: (jnp.maximum(i - 1, 0), 0)),
        _resident(lng.shape), _resident(lnb.shape),
        _resident(wq.shape), _resident(wk.shape), _resident(wv.shape), _resident(wg.shape),
        _resident(wz.shape), _resident(wga.shape), _resident(wgb.shape), _resident(wn.shape),
        _resident(par.shape), _resident(tri.shape), _resident(e_mat.shape),
    ]
    out_shape = [
        jax.ShapeDtypeStruct((n_blocks - 1, wide, ROW_BLOCK), BF16),
        jax.ShapeDtypeStruct((rows, wide), BF16),
        jax.ShapeDtypeStruct((n_blocks, wide, ROW_BLOCK), BF16),
        jax.ShapeDtypeStruct((rows, 3 * GDN_K_WIDTH), BF16),
        jax.ShapeDtypeStruct((rows, GDN_V_WIDTH), BF16),
        jax.ShapeDtypeStruct((rows, D_MODEL), BF16),
        jax.ShapeDtypeStruct((rows, D_MODEL), BF16),
        jax.ShapeDtypeStruct((rows, LANES), F32),
        jax.ShapeDtypeStruct((n_blocks * SUBLANES, LANES), F32),
    ]
    q_spec = pl.BlockSpec((1, wide, ROW_BLOCK), lambda i: (jnp.maximum(i - 1, 0), 0, 0))
    v_spec = pl.BlockSpec((1, wide, ROW_BLOCK), lambda i: (i, 0, 0))
    out_specs = [q_spec, row_spec(wide), v_spec, row_spec(3 * GDN_K_WIDTH),
                 row_spec(GDN_V_WIDTH), row_spec(D_MODEL), row_spec(D_MODEL), row_spec(LANES),
                 pl.BlockSpec((SUBLANES, LANES), lambda i: (i, 0))]
    return pl.pallas_call(
        _proj_body,
        grid=(n_blocks,),
        in_specs=in_specs,
        out_specs=out_specs,
        out_shape=out_shape,
        scratch_shapes=[pltpu.VMEM((1, LANES), F32)],
        compiler_params=pltpu.CompilerParams(dimension_semantics=("arbitrary",),
                                             vmem_limit_bytes=VMEM_LIMIT_BYTES),
        name="ln_in_proj",
    )(prefix, x, lng, lnb, wq, wk, wv, wg, wz, wga, wgb, wn, par, tri, e_mat)


def _attn_body(r_ref, qt_ref, k_ref, vt_ref, o_ref, s_scr, mx_scr, m_scr, acc_scr):
    h = pl.program_id(0)
    i = pl.program_id(1)
    q_t = jnp.concatenate([qt_ref[b] for b in range(Q_BLOCK // ROW_BLOCK)], axis=1)

    def produce(slot, j, mask=None):
        start = pl.multiple_of(j * ROW_BLOCK, ROW_BLOCK)
        s = _dot(k_ref[pl.ds(start, ROW_BLOCK), :], q_t)
        if mask is not None:
            s = jnp.where(mask, s, MASK_BIAS)
        s_scr[slot] = s
        mx_scr[slot] = jnp.max(s.reshape(ROW_BLOCK // SUBLANES, SUBLANES, Q_BLOCK), axis=0)

    def consume(slot, j):
        rj = r_ref[j * SUBLANES + h] * LOG2E
        m = m_scr[...]
        m_new = jnp.maximum(m, jnp.max(mx_scr[slot], axis=0, keepdims=True) - rj)
        p = jnp.exp2(s_scr[slot] - (m_new + rj))
        alpha = jnp.exp2(m - m_new)
        acc_scr[...] = alpha * acc_scr[...] + _dot(vt_ref[j], p.astype(BF16))
        m_scr[...] = m_new

    m_scr[...] = jnp.full(m_scr.shape, NEG_INIT, F32)
    acc_scr[...] = jnp.zeros(acc_scr.shape, F32)
    produce(0, 0)

    def pair(t, _):
        j = 2 * t
        produce(1, j + 1)
        consume(0, j)
        produce(0, j + 2)
        consume(1, j + 1)
        return 0

    lax.fori_loop(0, i, pair, 0)
    first_diag = 2 * i + 1
    key = lax.broadcasted_iota(jnp.int32, (ROW_BLOCK, Q_BLOCK), 0)
    qry = lax.broadcasted_iota(jnp.int32, (ROW_BLOCK, Q_BLOCK), 1)
    produce(1, first_diag, qry >= key)
    consume(0, first_diag - 1)
    produce(0, first_diag + 1, qry >= key + ROW_BLOCK)
    consume(1, first_diag)
    consume(0, first_diag + 1)

    acc = acc_scr[...]
    denom = acc[FOX_HEAD_DIM:FOX_HEAD_DIM + 1, :]
    o_ref[...] = (acc / denom).T.astype(o_ref.dtype)


def _attn_call(r_flat, qt, ka, vt):
    n_kv = vt.shape[0]
    n_q = qt.shape[0] * ROW_BLOCK // Q_BLOCK
    return pl.pallas_call(
        _attn_body,
        grid=(FOX_HEADS, n_q),
        in_specs=[
            pl.BlockSpec(memory_space=pltpu.SMEM),
            pl.BlockSpec((Q_BLOCK // ROW_BLOCK, HEAD_SLOT, ROW_BLOCK), lambda h, i: (i, h, 0)),
            pl.BlockSpec((n_kv * ROW_BLOCK, HEAD_SLOT), lambda h, i: (0, h)),
            pl.BlockSpec((n_kv, HEAD_SLOT, ROW_BLOCK), lambda h, i: (0, h, 0)),
        ],
        out_specs=pl.BlockSpec((Q_BLOCK, HEAD_SLOT), lambda h, i: (i, h)),
        out_shape=jax.ShapeDtypeStruct((n_q * Q_BLOCK, FOX_HEADS * HEAD_SLOT), BF16),
        scratch_shapes=[pltpu.VMEM((2, ROW_BLOCK, Q_BLOCK), F32),
                        pltpu.VMEM((2, SUBLANES, Q_BLOCK), F32),
                        pltpu.VMEM((1, Q_BLOCK), F32),
                        pltpu.VMEM((HEAD_SLOT, Q_BLOCK), F32)],
        compiler_params=pltpu.CompilerParams(dimension_semantics=("arbitrary", "arbitrary"),
                                             vmem_limit_bytes=VMEM_LIMIT_BYTES),
        name="fox_attention",
    )(r_flat, qt, ka, vt)


def _gdn_body(qkv_ref, hist_ref, small_ref, z_ref, convw_ref, normw_ref, y_ref,
              state_ref, xn_ref, u_scr, w_scr, attn_scr, qd_scr, kd_scr, vnew_scr):
    i = pl.program_id(0)

    @pl.when(i == 0)
    def _():
        state_ref[...] = jnp.zeros_like(state_ref)
        vnew_scr[...] = jnp.zeros_like(vnew_scr)

    hist = jnp.where(i > 0, hist_ref[...].astype(F32), 0.0)
    cur = qkv_ref[...].astype(F32)
    ext = jnp.concatenate([hist, cur], axis=0)
    conv = jnp.zeros_like(cur)
    for t in range(CONV_WIDTH):
        off = SUBLANES - (CONV_WIDTH - 1) + t
        conv = conv + convw_ref[t:t + 1, :] * ext[off:off + ROW_BLOCK, :]
    act = _silu(conv)
    for hd in range(2 * GDN_HEADS):
        sl = slice(hd * GDN_KEY_DIM, (hd + 1) * GDN_KEY_DIM)
        blk = act[:, sl]
        scale = lax.rsqrt(jnp.sum(blk * blk, axis=-1, keepdims=True) + NORM_EPS)
        if hd < GDN_HEADS:
            scale = scale * (GDN_KEY_DIM ** -0.5)
        xn_ref[:, sl] = blk * scale
    xn_ref[:, 2 * GDN_K_WIDTH:] = act[:, 2 * GDN_K_WIDTH:]

    ri = lax.broadcasted_iota(jnp.int32, (SUPER, SUPER), 0)
    ci = lax.broadcasted_iota(jnp.int32, (SUPER, SUPER), 1)
    same = (ri // CHUNK) == (ci // CHUNK)
    incl = jnp.logical_and(same, ri >= ci)
    strict = jnp.logical_and(same, ri > ci)
    eye = (ri == ci).astype(F32)

    def head_slices(hd):
        qs = slice(hd * GDN_KEY_DIM, (hd + 1) * GDN_KEY_DIM)
        ks = slice(GDN_K_WIDTH + hd * GDN_KEY_DIM, GDN_K_WIDTH + (hd + 1) * GDN_KEY_DIM)
        vs = slice(2 * GDN_K_WIDTH + hd * GDN_VAL_DIM, 2 * GDN_K_WIDTH + (hd + 1) * GDN_VAL_DIM)
        return qs, ks, vs

    for sc in range(ROW_BLOCK // SUPER):
        rows = slice(sc * SUPER, (sc + 1) * SUPER)
        sm = small_ref[rows, :]
        sm_t = sm.T
        for h0 in range(0, GDN_HEADS, HEAD_GROUP):
            heads = range(h0, h0 + HEAD_GROUP)
            pw, t_inv, keep = {}, {}, {}
            for hd in heads:
                qs, ks, vs = head_slices(hd)
                qc = xn_ref[rows, qs]
                kc = xn_ref[rows, ks]
                vc = xn_ref[rows, vs]
                gcol = sm[:, G_LANE0 + hd:G_LANE0 + hd + 1]
                bcol = sm[:, BETA_LANE0 + hd:BETA_LANE0 + hd + 1]
                grow = sm_t[G_LANE0 + hd:G_LANE0 + hd + 1, :]
                glast = jnp.concatenate(
                    [jnp.broadcast_to(gcol[(c + 1) * CHUNK - 1:(c + 1) * CHUNK, :], (CHUNK, 1))
                     for c in range(SUPER // CHUNK)], axis=0)
                decay = jnp.exp(jnp.where(incl, gcol - grow, -jnp.inf))
                egc = jnp.exp(gcol)
                kb = kc * bcol
                kcb = kc.astype(BF16)
                a_mat = jnp.where(strict, _nt_dot(kb.astype(BF16), kcb) * decay, 0.0)
                pw[hd] = -a_mat
                t_inv[hd] = eye + pw[hd]
                attn_scr[hd, rows, :] = jnp.where(
                    incl, _nt_dot(qc.astype(BF16), kcb) * decay, 0.0).astype(BF16)
                qd_scr[rows, qs] = (qc * egc).astype(BF16)
                kd_scr[rows, qs] = (kc * jnp.exp(glast - gcol)).astype(BF16)
                keep[hd] = jnp.concatenate([vc * bcol, kb * egc], axis=1).astype(BF16)
            for _k in range(int(math.log2(CHUNK)) - 1):
                for hd in heads:
                    pwb = pw[hd].astype(BF16)
                    pw[hd] = _dot(pwb, pwb)
                for hd in heads:
                    t_inv[hd] = t_inv[hd] + _dot(t_inv[hd].astype(BF16), pw[hd].astype(BF16))
            for hd in heads:
                qs, _, _ = head_slices(hd)
                uw = _dot(t_inv[hd].astype(BF16), keep[hd])
                u_scr[rows, qs] = uw[:, :GDN_VAL_DIM]
                w_scr[rows, qs] = uw[:, GDN_VAL_DIM:].astype(BF16)

    normw = normw_ref[...]

    def chunk_step(c, _):
        r0 = pl.multiple_of(c * CHUNK, CHUNK)
        in_sup = pl.multiple_of(lax.rem(c, SUPER // CHUNK) * CHUNK, CHUNK)
        g_tail = small_ref[pl.ds(r0 + CHUNK - SUBLANES, SUBLANES), :]
        g_end = g_tail[SUBLANES - 1:SUBLANES, :]
        states, sbs, vnbs, outs = {}, {}, {}, {}
        for hd in range(GDN_HEADS):
            qs, _, _ = head_slices(hd)
            states[hd] = state_ref[hd]
            sbs[hd] = states[hd].astype(BF16)
            v_new = u_scr[pl.ds(r0, CHUNK), qs] - _dot(w_scr[pl.ds(r0, CHUNK), qs], sbs[hd])
            vnbs[hd] = v_new.astype(BF16)
            vnew_scr[hd, pl.ds(in_sup, CHUNK), :] = vnbs[hd]
        for hd in range(GDN_HEADS):
            qs, _, _ = head_slices(hd)
            outs[hd] = (_dot(qd_scr[pl.ds(r0, CHUNK), qs], sbs[hd])
                        + _dot(attn_scr[hd, pl.ds(r0, CHUNK), :], vnew_scr[hd]))
        for hd in range(GDN_HEADS):
            qs, _, _ = head_slices(hd)
            gl = jnp.exp(g_end[:, G_LANE0 + hd:G_LANE0 + hd + 1])
            state_ref[hd] = states[hd] * gl + lax.dot_general(
                kd_scr[pl.ds(r0, CHUNK), qs], vnbs[hd], (((0,), (0,)), ((), ())),
                preferred_element_type=F32)
        for hd in range(GDN_HEADS):
            qs, _, _ = head_slices(hd)
            o = outs[hd]
            o = o * lax.rsqrt(jnp.mean(o * o, axis=-1, keepdims=True) + NORM_EPS) * normw
            zc = z_ref[pl.ds(r0, CHUNK), qs].astype(F32)
            y_ref[pl.ds(r0, CHUNK), qs] = (o * _silu(zc)).astype(y_ref.dtype)
        return 0

    lax.fori_loop(0, CHUNKS_PER_BLOCK, chunk_step, 0)


def _gdn_call(qkv, small, z, convw, normw):
    rows = qkv.shape[0]
    n_blocks = rows // ROW_BLOCK
    hist_blocks_per_row_block = ROW_BLOCK // SUBLANES
    return pl.pallas_call(
        _gdn_body,
        grid=(n_blocks,),
        in_specs=[
            pl.BlockSpec((ROW_BLOCK, 3 * GDN_K_WIDTH), lambda i: (i, 0)),
            pl.BlockSpec((SUBLANES, 3 * GDN_K_WIDTH),
                         lambda i: (jnp.maximum(i * hist_blocks_per_row_block - 1, 0), 0)),
            pl.BlockSpec((ROW_BLOCK, LANES), lambda i: (i, 0)),
            pl.BlockSpec((ROW_BLOCK, GDN_V_WIDTH), lambda i: (i, 0)),
            _resident(convw.shape), _resident(normw.shape),
        ],
        out_specs=pl.BlockSpec((ROW_BLOCK, GDN_V_WIDTH), lambda i: (i, 0)),
        out_shape=jax.ShapeDtypeStruct((rows, GDN_V_WIDTH), BF16),
        scratch_shapes=[pltpu.VMEM((GDN_HEADS, GDN_KEY_DIM, GDN_VAL_DIM), F32),
                        pltpu.VMEM((ROW_BLOCK, 3 * GDN_K_WIDTH), F32),
                        pltpu.VMEM((ROW_BLOCK, GDN_V_WIDTH), F32),
                        pltpu.VMEM((ROW_BLOCK, GDN_K_WIDTH), BF16),
                        pltpu.VMEM((GDN_HEADS, ROW_BLOCK, SUPER), BF16),
                        pltpu.VMEM((ROW_BLOCK, GDN_K_WIDTH), BF16),
                        pltpu.VMEM((ROW_BLOCK, GDN_K_WIDTH), BF16),
                        pltpu.VMEM((GDN_HEADS, SUPER, GDN_VAL_DIM), BF16)],
        compiler_params=pltpu.CompilerParams(dimension_semantics=("arbitrary",),
                                             vmem_limit_bytes=VMEM_LIMIT_BYTES),
        name="gated_delta_rule",
    )(qkv, qkv, small, z, convw, normw)


def _out_body(x_ref, ya_ref, yb_ref, ga_ref, gb_ref, lng_ref, lnb_ref, woa_ref, wob_ref, wo_ref,
              ln1g_ref, ln1b_ref, wgate_ref, wup_ref, wdown_ref, ln2g_ref, ln2b_ref, o_ref):
    h0 = _layer_norm(x_ref[...], lng_ref[...], lnb_ref[...])
    pa = _dot(ya_ref[...], woa_ref[...])
    pb = _dot(yb_ref[...], wob_ref[...])
    mixed = _sigmoid(ga_ref[...].astype(F32)) * pa + _sigmoid(gb_ref[...].astype(F32)) * pb
    h1 = _layer_norm(ALPHA * h0 + _dot(mixed.astype(BF16), wo_ref[...]), ln1g_ref[...], ln1b_ref[...])
    h1b = h1.astype(BF16)
    d_ff = wgate_ref.shape[1]
    half = d_ff // 2
    ffn = jnp.zeros_like(h1)
    for c in range(2):
        sl = slice(c * half, (c + 1) * half)
        gate = _dot(h1b, wgate_ref[:, sl])
        up = _dot(h1b, wup_ref[:, sl])
        ffn = ffn + _dot((_silu(gate) * up).astype(BF16), wdown_ref[sl, :])
    o_ref[...] = _layer_norm(ALPHA * h1 + ffn, ln2g_ref[...], ln2b_ref[...])


def _out_call(x, ya, yb, ga, gb, lng, lnb, woa, wob, wo, ln1g, ln1b, wgate, wup, wdown, ln2g, ln2b):
    seq = x.shape[0]
    n_blocks = seq // ROW_BLOCK
    main = lambda i: (i, 0)
    shifted = lambda i: (i + 1, 0)
    res = [lng, lnb, woa, wob, wo, ln1g, ln1b, wgate, wup, wdown, ln2g, ln2b]
    return pl.pallas_call(
        _out_body,
        grid=(n_blocks,),
        in_specs=[
            pl.BlockSpec((ROW_BLOCK, D_MODEL), main),
            pl.BlockSpec((ROW_BLOCK, ya.shape[1]), main),
            pl.BlockSpec((ROW_BLOCK, GDN_V_WIDTH), shifted),
            pl.BlockSpec((ROW_BLOCK, D_MODEL), shifted),
            pl.BlockSpec((ROW_BLOCK, D_MODEL), shifted),
        ] + [_resident(a.shape) for a in res],
        out_specs=pl.BlockSpec((ROW_BLOCK, D_MODEL), main),
        out_shape=jax.ShapeDtypeStruct((seq, D_MODEL), F32),
        compiler_params=pltpu.CompilerParams(dimension_semantics=("parallel",),
                                             vmem_limit_bytes=VMEM_LIMIT_BYTES),
        name="merge_out_ffn",
    )(x, ya, yb, ga, gb, *res)


def _slot_columns(w, scale=1.0):
    d = w.shape[0]
    w = (w * scale).reshape(d, FOX_HEADS, FOX_HEAD_DIM)
    w = jnp.pad(w, ((0, 0), (0, 0), (0, HEAD_SLOT - FOX_HEAD_DIM)))
    return w.reshape(d, FOX_HEADS * HEAD_SLOT)


def _slot_rows(w):
    d = w.shape[1]
    w = w.reshape(FOX_HEADS, FOX_HEAD_DIM, d)
    w = jnp.pad(w, ((0, 0), (0, HEAD_SLOT - FOX_HEAD_DIM), (0, 0)))
    return w.reshape(FOX_HEADS * HEAD_SLOT, d)


def _constants():
    wide = FOX_HEADS * HEAD_SLOT
    r = np.arange(ROW_BLOCK)
    tri_full = (r[:, None] >= r[None, :])
    tri_chunk = tri_full & ((r[:, None] // CHUNK) == (r[None, :] // CHUNK))
    tri = np.concatenate([tri_full, tri_chunk], axis=0).astype(np.float32)
    e_mat = np.zeros((3 * LANES, wide), np.float32)
    for hd in range(FOX_HEADS):
        for piece in range(3):
            e_mat[piece * LANES + LOGF_LANE0 + hd, hd * HEAD_SLOT + BIAS_LANE + piece] = 1.0
    return jnp.asarray(tri, BF16), jnp.asarray(e_mat, BF16)


def kernel(x, meta_tokens, ln_in_g, ln_in_b, w_in, b_f, conv_w, a_log, dt_bias, gdn_norm_w,
           w_out_a, w_out_b, w_o, ln1_g, ln1_b, w_gate, w_up, w_down, ln2_g, ln2_b):
    bsz, seq, d = x.shape
    assert bsz == 1 and d == D_MODEL and seq % Q_BLOCK == 0 and w_in.shape[0] == DEPTH
    x2 = x[0]
    w = w_in[0]
    o = 0
    cols = {}
    for name, width in (("q_a", FOX_WIDTH), ("k_a", FOX_WIDTH), ("v_a", FOX_WIDTH), ("f_a", FOX_HEADS),
                        ("qkv_b", 3 * GDN_K_WIDTH), ("a_b", GDN_HEADS), ("b_b", GDN_HEADS),
                        ("z_b", GDN_V_WIDTH), ("gate_a", D_MODEL), ("gate_b", D_MODEL)):
        cols[name] = w[:, o:o + width]
        o += width
    assert o == w.shape[1]

    wq = _slot_columns(cols["q_a"], FOX_HEAD_DIM ** -0.5 * LOG2E).T.astype(BF16)
    wk = _slot_columns(cols["k_a"]).astype(BF16)
    wv = _slot_columns(cols["v_a"]).T.astype(BF16)
    wn = jnp.zeros((d, LANES), F32)
    wn = wn.at[:, LOGF_LANE0:LOGF_LANE0 + FOX_HEADS].set(cols["f_a"])
    wn = wn.at[:, G_LANE0:G_LANE0 + GDN_HEADS].set(cols["a_b"])
    wn = wn.at[:, BETA_LANE0:BETA_LANE0 + GDN_HEADS].set(cols["b_b"]).astype(BF16)
    par = jnp.zeros((SUBLANES, LANES), F32)
    par = par.at[0, LOGF_LANE0:LOGF_LANE0 + FOX_HEADS].set(b_f[0])
    par = par.at[1, G_LANE0:G_LANE0 + GDN_HEADS].set(a_log[0])
    par = par.at[2, G_LANE0:G_LANE0 + GDN_HEADS].set(dt_bias[0])
    tri, e_mat = _constants()
    prefix = jnp.zeros((PRE, d), F32).at[LEAD:].set(meta_tokens.astype(F32))
    row = lambda v: v.reshape(1, -1).astype(F32)

    qt, ka, vt, qkv, z, ga, gb, small, r_blk = _proj_call(
        prefix, x2, row(ln_in_g), row(ln_in_b), wq, wk, wv, cols["qkv_b"].astype(BF16),
        cols["z_b"].astype(BF16), cols["gate_a"].astype(BF16), cols["gate_b"].astype(BF16),
        wn, par, tri, e_mat)

    r_flat = r_blk[:, :SUBLANES].reshape(-1, SUBLANES, SUBLANES)[:, 0, :].reshape(-1)
    ya = _attn_call(r_flat, qt, ka, vt)
    yb = _gdn_call(qkv, small, z, conv_w[0].astype(F32), row(gdn_norm_w[0]))
    out = _out_call(x2, ya, yb, ga, gb, row(ln_in_g), row(ln_in_b),
                    _slot_rows(w_out_a[0]).astype(BF16), w_out_b[0].astype(BF16), w_o[0].astype(BF16),
                    row(ln1_g[0]), row(ln1_b[0]), w_gate[0].astype(BF16), w_up[0].astype(BF16),
                    w_down[0].astype(BF16), row(ln2_g[0]), row(ln2_b[0]))
    return out[None]
```

```python
import functools
import math

import jax
import jax.numpy as jnp
import numpy as np
from jax import lax
from jax.experimental import pallas as pl
from jax.experimental.pallas import tpu as pltpu

D_MODEL = 1024
CHUNK = 64
N_META = 16
FOX_HEADS = 8
FOX_HEAD_DIM = 64
FOX_WIDTH = FOX_HEADS * FOX_HEAD_DIM
GDN_HEADS = 8
GDN_KEY_DIM = 128
GDN_VAL_DIM = 128
GDN_K_WIDTH = GDN_HEADS * GDN_KEY_DIM
GDN_V_WIDTH = GDN_HEADS * GDN_VAL_DIM
CONV_WIDTH = 4
DEPTH = 1
ALPHA = (2.0 * DEPTH) ** 0.25
LN_EPS = 1e-5
NORM_EPS = 1e-6

LANES = 128
SUBLANES = 8
VMEM_LIMIT_BYTES = 56 * 1024 * 1024

ROW_BLOCK = 512
PRE = ROW_BLOCK
LEAD = PRE - N_META
CHUNKS_PER_BLOCK = ROW_BLOCK // CHUNK
Q_BLOCK = 2 * ROW_BLOCK
SUPER = 256
HEAD_GROUP = 4
HEAD_SLOT = LANES
BIAS_LANE = FOX_HEAD_DIM
V_ROWS = FOX_HEAD_DIM + 16
MASK_BIAS = -1e30
NEG_INIT = -1e30
LOG2E = math.log2(math.e)
F32 = jnp.float32
BF16 = jnp.bfloat16

LOGF_LANE0 = 0
G_LANE0 = 8
BETA_LANE0 = 16


def _nt_dot(a, b):
    return lax.dot_general(a, b, (((1,), (1,)), ((), ())), preferred_element_type=F32)


def _dot(a, b):
    return jnp.dot(a, b, preferred_element_type=F32)


def _layer_norm(x, g, b):
    mu = jnp.mean(x, axis=-1, keepdims=True)
    xc = x - mu
    var = jnp.mean(xc * xc, axis=-1, keepdims=True)
    return xc * lax.rsqrt(var + LN_EPS) * g + b


def _softplus(x):
    return jnp.maximum(x, 0.0) + jnp.log(1.0 + jnp.exp(-jnp.abs(x)))


def _sigmoid(x):
    return 1.0 / (1.0 + jnp.exp(-x))


def _silu(x):
    return x * _sigmoid(x)


def _split3(x):
    hi = x.astype(BF16)
    r1 = x - hi.astype(F32)
    mid = r1.astype(BF16)
    lo = (r1 - mid.astype(F32)).astype(BF16)
    return hi, mid, lo


def _resident(shape):
    nd = len(shape)
    return pl.BlockSpec(shape, lambda *_: (0,) * nd, pipeline_mode=pl.Buffered(1))


def _proj_body(prefix_ref, x_ref, lng_ref, lnb_ref, wq_ref, wk_ref, wv_ref, wg_ref, wz_ref, wga_ref,
               wgb_ref, wn_ref, par_ref, tri_ref, e_ref,
               qt_ref, ka_ref, vt_ref, qkv_ref, z_ref, ga_ref, gb_ref, small_ref, r_ref,
               carry_ref):
    i = pl.program_id(0)

    @pl.when(i == 0)
    def _():
        carry_ref[...] = jnp.zeros_like(carry_ref)

    src = jnp.where(i == 0, prefix_ref[...], x_ref[...])
    h = _layer_norm(src, lng_ref[...], lnb_ref[...])
    row = lax.broadcasted_iota(jnp.int32, (ROW_BLOCK, 1), 0)
    valid = jnp.logical_or(i > 0, row >= LEAD)
    hb = jnp.where(valid, h, 0.0).astype(BF16)

    nar = _dot(hb, wn_ref[...])
    lane = lax.broadcasted_iota(jnp.int32, (1, LANES), 1)
    b_f = par_ref[0:1, :]
    a_log = par_ref[1:2, :]
    dt_bias = par_ref[2:3, :]
    xf = nar + b_f
    logf = jnp.minimum(xf, 0.0) - jnp.log(1.0 + jnp.exp(-jnp.abs(xf)))
    g = -jnp.exp(a_log) * _softplus(nar + dt_bias)
    beta = _sigmoid(nar)
    is_logf = lane < G_LANE0
    is_g = jnp.logical_and(lane >= G_LANE0, lane < BETA_LANE0)
    vals = jnp.where(is_logf, logf, jnp.where(is_g, g, 0.0))

    hi, mid, lo = _split3(vals)
    tri = tri_ref[...]
    cs = _dot(tri, hi) + _dot(tri, mid) + _dot(tri, lo)
    d_blk = cs[:ROW_BLOCK]
    gc = cs[ROW_BLOCK:]
    carry = carry_ref[...]
    r_ref[...] = jnp.broadcast_to(carry, r_ref.shape)
    carry_ref[...] = carry + d_blk[ROW_BLOCK - 1:ROW_BLOCK, :]
    small_ref[...] = jnp.where(is_g, gc, jnp.where(lane >= BETA_LANE0, beta, d_blk))

    nd = jnp.where(valid, -LOG2E * d_blk, MASK_BIAS)
    nd = jnp.where(is_logf, nd, 0.0)
    phi, pmid, plo = _split3(nd)
    pieces = jnp.concatenate([phi, pmid, plo], axis=1)
    qt_ref[0] = _nt_dot(wq_ref[...], hb).astype(BF16)
    ka_ref[...] = (_dot(hb, wk_ref[...]) + _dot(pieces, e_ref[...])).astype(BF16)
    vt_ref[0] = _nt_dot(wv_ref[...], hb).astype(BF16)

    for c in range(3):
        sl = slice(c * GDN_K_WIDTH, (c + 1) * GDN_K_WIDTH)
        qkv_ref[:, sl] = _dot(hb, wg_ref[:, sl]).astype(BF16)
    z_ref[...] = _dot(hb, wz_ref[...]).astype(BF16)
    ga_ref[...] = _dot(hb, wga_ref[...]).astype(BF16)
    gb_ref[...] = _dot(hb, wgb_ref[...]).astype(BF16)


def _proj_call(prefix, x, lng, lnb, wq, wk, wv, wg, wz, wga, wgb, wn, par, tri, e_mat):
    seq = x.shape[0]
    n_blocks = seq // ROW_BLOCK + 1
    rows = n_blocks * ROW_BLOCK
    wide = FOX_HEADS * HEAD_SLOT

    def row_spec(width):
        return pl.BlockSpec((ROW_BLOCK, width), lambda i: (i, 0))

    in_specs = [
        _resident(prefix.shape),
        pl.BlockSpec((ROW_BLOCK, D_MODEL), lambda i: (jnp.maximum(i - 1, 0), 0)),
        _resident(lng.shape), _resident(lnb.shape),
        _resident(wq.shape), _resident(wk.shape), _resident(wv.shape), _resident(wg.shape),
        _resident(wz.shape), _resident(wga.shape), _resident(wgb.shape), _resident(wn.shape),
        _resident(par.shape), _resident(tri.shape), _resident(e_mat.shape),
    ]
    out_shape = [
        jax.ShapeDtypeStruct((n_blocks - 1, FOX_WIDTH, ROW_BLOCK), BF16),
        jax.ShapeDtypeStruct((rows, wide), BF16),
        jax.ShapeDtypeStruct((n_blocks, FOX_WIDTH, ROW_BLOCK), BF16),
        jax.ShapeDtypeStruct((rows, 3 * GDN_K_WIDTH), BF16),
        jax.ShapeDtypeStruct((rows, GDN_V_WIDTH), BF16),
        jax.ShapeDtypeStruct((rows, D_MODEL), BF16),
        jax.ShapeDtypeStruct((rows, D_MODEL), BF16),
        jax.ShapeDtypeStruct((rows, LANES), F32),
        jax.ShapeDtypeStruct((n_blocks * SUBLANES, LANES), F32),
    ]
    q_spec = pl.BlockSpec((1, FOX_WIDTH, ROW_BLOCK), lambda i: (jnp.maximum(i - 1, 0), 0, 0))
    v_spec = pl.BlockSpec((1, FOX_WIDTH, ROW_BLOCK), lambda i: (i, 0, 0))
    out_specs = [q_spec, row_spec(wide), v_spec, row_spec(3 * GDN_K_WIDTH),
                 row_spec(GDN_V_WIDTH), row_spec(D_MODEL), row_spec(D_MODEL), row_spec(LANES),
                 pl.BlockSpec((SUBLANES, LANES), lambda i: (i, 0))]
    return pl.pallas_call(
        _proj_body,
        grid=(n_blocks,),
        in_specs=in_specs,
        out_specs=out_specs,
        out_shape=out_shape,
        scratch_shapes=[pltpu.VMEM((1, LANES), F32)],
        compiler_params=pltpu.CompilerParams(dimension_semantics=("arbitrary",),
                                             vmem_limit_bytes=VMEM_LIMIT_BYTES),
        name="ln_in_proj",
    )(prefix, x, lng, lnb, wq, wk, wv, wg, wz, wga, wgb, wn, par, tri, e_mat)


def _attn_body(r_ref, qt_ref, k_ref, vt_ref, o_ref, s_scr, mx_scr, m_scr, acc_scr):
    h = pl.program_id(0)
    i = pl.program_id(1)
    q_t = jnp.concatenate([qt_ref[b] for b in range(Q_BLOCK // ROW_BLOCK)], axis=1)
    pad_row = lax.broadcasted_iota(jnp.int32, (HEAD_SLOT - FOX_HEAD_DIM, Q_BLOCK), 0)
    q_t = jnp.concatenate([q_t, jnp.where(pad_row < 3, 1.0, 0.0).astype(BF16)], axis=0)
    one_row = lax.broadcasted_iota(jnp.int32, (V_ROWS - FOX_HEAD_DIM, ROW_BLOCK), 0)
    v_ones = jnp.where(one_row == 0, 1.0, 0.0).astype(BF16)

    def produce(slot, j, mask=None):
        start = pl.multiple_of(j * ROW_BLOCK, ROW_BLOCK)
        s = _dot(k_ref[pl.ds(start, ROW_BLOCK), :], q_t)
        if mask is not None:
            s = jnp.where(mask, s, MASK_BIAS)
        s_scr[slot] = s
        mx_scr[slot] = jnp.max(s.reshape(ROW_BLOCK // SUBLANES, SUBLANES, Q_BLOCK), axis=0)

    def consume(slot, j):
        rj = r_ref[j * SUBLANES + h] * LOG2E
        m = m_scr[...]
        m_new = jnp.maximum(m, jnp.max(mx_scr[slot], axis=0, keepdims=True) - rj)
        p = jnp.exp2(s_scr[slot] - (m_new + rj))
        alpha = jnp.exp2(m - m_new)
        v_t = jnp.concatenate([vt_ref[j], v_ones], axis=0)
        acc_scr[...] = alpha * acc_scr[...] + _dot(v_t, p.astype(BF16))
        m_scr[...] = m_new

    m_scr[...] = jnp.full(m_scr.shape, NEG_INIT, F32)
    acc_scr[...] = jnp.zeros(acc_scr.shape, F32)
    produce(0, 0)

    def pair(t, _):
        j = 2 * t
        produce(1, j + 1)
        consume(0, j)
        produce(0, j + 2)
        consume(1, j + 1)
        return 0

    lax.fori_loop(0, i, pair, 0)
    first_diag = 2 * i + 1
    key = lax.broadcasted_iota(jnp.int32, (ROW_BLOCK, Q_BLOCK), 0)
    qry = lax.broadcasted_iota(jnp.int32, (ROW_BLOCK, Q_BLOCK), 1)
    produce(1, first_diag, qry >= key)
    consume(0, first_diag - 1)
    produce(0, first_diag + 1, qry >= key + ROW_BLOCK)
    consume(1, first_diag)
    consume(0, first_diag + 1)

    acc = acc_scr[...]
    denom = acc[FOX_HEAD_DIM:FOX_HEAD_DIM + 1, :]
    out_t = jnp.concatenate([acc[:FOX_HEAD_DIM] / denom,
                             jnp.zeros((HEAD_SLOT - FOX_HEAD_DIM, Q_BLOCK), F32)], axis=0)
    o_ref[...] = out_t.T.astype(o_ref.dtype)


def _attn_call(r_flat, qt, ka, vt):
    n_kv = vt.shape[0]
    n_q = qt.shape[0] * ROW_BLOCK // Q_BLOCK
    return pl.pallas_call(
        _attn_body,
        grid=(FOX_HEADS, n_q),
        in_specs=[
            pl.BlockSpec(memory_space=pltpu.SMEM),
            pl.BlockSpec((Q_BLOCK // ROW_BLOCK, FOX_HEAD_DIM, ROW_BLOCK), lambda h, i: (i, h, 0)),
            pl.BlockSpec((n_kv * ROW_BLOCK, HEAD_SLOT), lambda h, i: (0, h)),
            pl.BlockSpec((n_kv, FOX_HEAD_DIM, ROW_BLOCK), lambda h, i: (0, h, 0)),
        ],
        out_specs=pl.BlockSpec((Q_BLOCK, HEAD_SLOT), lambda h, i: (i, h)),
        out_shape=jax.ShapeDtypeStruct((n_q * Q_BLOCK, FOX_HEADS * HEAD_SLOT), BF16),
        scratch_shapes=[pltpu.VMEM((2, ROW_BLOCK, Q_BLOCK), F32),
                        pltpu.VMEM((2, SUBLANES, Q_BLOCK), F32),
                        pltpu.VMEM((1, Q_BLOCK), F32),
                        pltpu.VMEM((V_ROWS, Q_BLOCK), F32)],
        compiler_params=pltpu.CompilerParams(dimension_semantics=("arbitrary", "arbitrary"),
                                             vmem_limit_bytes=VMEM_LIMIT_BYTES),
        name="fox_attention",
    )(r_flat, qt, ka, vt)


def _gdn_body(qkv_ref, hist_ref, small_ref, z_ref, convw_ref, normw_ref, y_ref,
              state_ref, xn_ref, u_scr, w_scr, attn_scr, qd_scr, kd_scr, vnew_scr):
    i = pl.program_id(0)

    @pl.when(i == 0)
    def _():
        state_ref[...] = jnp.zeros_like(state_ref)
        vnew_scr[...] = jnp.zeros_like(vnew_scr)

    hist = jnp.where(i > 0, hist_ref[...].astype(F32), 0.0)
    cur = qkv_ref[...].astype(F32)
    ext = jnp.concatenate([hist, cur], axis=0)
    conv = convw_ref[CONV_WIDTH - 1:CONV_WIDTH, :] * cur
    for t in range(CONV_WIDTH - 1):
        off = SUBLANES - (CONV_WIDTH - 1) + t
        conv = conv + convw_ref[t:t + 1, :] * ext[off:off + ROW_BLOCK, :]
    act = _silu(conv)
    for hd in range(2 * GDN_HEADS):
        sl = slice(hd * GDN_KEY_DIM, (hd + 1) * GDN_KEY_DIM)
        blk = act[:, sl]
        scale = lax.rsqrt(jnp.sum(blk * blk, axis=-1, keepdims=True) + NORM_EPS)
        if hd < GDN_HEADS:
            scale = scale * (GDN_KEY_DIM ** -0.5)
        xn_ref[:, sl] = blk * scale
    xn_ref[:, 2 * GDN_K_WIDTH:] = act[:, 2 * GDN_K_WIDTH:]

    ri = lax.broadcasted_iota(jnp.int32, (SUPER, SUPER), 0)
    ci = lax.broadcasted_iota(jnp.int32, (SUPER, SUPER), 1)
    same = (ri // CHUNK) == (ci // CHUNK)
    incl = jnp.logical_and(same, ri >= ci)
    strict = jnp.logical_and(same, ri > ci)
    eye = (ri == ci).astype(F32)

    def head_slices(hd):
        qs = slice(hd * GDN_KEY_DIM, (hd + 1) * GDN_KEY_DIM)
        ks = slice(GDN_K_WIDTH + hd * GDN_KEY_DIM, GDN_K_WIDTH + (hd + 1) * GDN_KEY_DIM)
        vs = slice(2 * GDN_K_WIDTH + hd * GDN_VAL_DIM, 2 * GDN_K_WIDTH + (hd + 1) * GDN_VAL_DIM)
        return qs, ks, vs

    for sc in range(ROW_BLOCK // SUPER):
        rows = slice(sc * SUPER, (sc + 1) * SUPER)
        sm = small_ref[rows, :]
        sm_t = sm.T
        for h0 in range(0, GDN_HEADS, HEAD_GROUP):
            heads = range(h0, h0 + HEAD_GROUP)
            pw, pwb, t_inv, keep = {}, {}, {}, {}
            for hd in heads:
                qs, ks, vs = head_slices(hd)
                qc = xn_ref[rows, qs]
                kc = xn_ref[rows, ks]
                vc = xn_ref[rows, vs]
                gcol = sm[:, G_LANE0 + hd:G_LANE0 + hd + 1]
                bcol = sm[:, BETA_LANE0 + hd:BETA_LANE0 + hd + 1]
                grow = sm_t[G_LANE0 + hd:G_LANE0 + hd + 1, :]
                glast = jnp.concatenate(
                    [jnp.broadcast_to(gcol[(c + 1) * CHUNK - 1:(c + 1) * CHUNK, :], (CHUNK, 1))
                     for c in range(SUPER // CHUNK)], axis=0)
                decay = jnp.exp2(LOG2E * gcol - LOG2E * grow)
                egc = jnp.exp(gcol)
                kb = kc * bcol
                kcb = kc.astype(BF16)
                pw[hd] = jnp.where(strict, _nt_dot((-kb).astype(BF16), kcb) * decay, 0.0)
                pwb[hd] = pw[hd].astype(BF16)
                t_inv[hd] = eye + pw[hd]
                attn_scr[hd, rows, :] = jnp.where(
                    incl, _nt_dot(qc.astype(BF16), kcb) * decay, 0.0).astype(BF16)
                qd_scr[rows, qs] = (qc * egc).astype(BF16)
                kd_scr[rows, qs] = (kc * jnp.exp(glast - gcol)).astype(BF16)
                keep[hd] = jnp.concatenate([vc * bcol, kb * egc], axis=1).astype(BF16)
            for _k in range(int(math.log2(CHUNK)) - 1):
                for hd in heads:
                    pwb[hd] = _dot(pwb[hd], pwb[hd]).astype(BF16)
                for hd in heads:
                    t_inv[hd] = t_inv[hd] + _dot(t_inv[hd].astype(BF16), pwb[hd])
            for hd in heads:
                qs, _, _ = head_slices(hd)
                uw = _dot(t_inv[hd].astype(BF16), keep[hd])
                u_scr[rows, qs] = uw[:, :GDN_VAL_DIM]
                w_scr[rows, qs] = uw[:, GDN_VAL_DIM:].astype(BF16)

    normw = normw_ref[...]

    def chunk_step(c, _):
        r0 = pl.multiple_of(c * CHUNK, CHUNK)
        in_sup = pl.multiple_of(lax.rem(c, SUPER // CHUNK) * CHUNK, CHUNK)
        g_tail = small_ref[pl.ds(r0 + CHUNK - SUBLANES, SUBLANES), :]
        g_end = g_tail[SUBLANES - 1:SUBLANES, :]
        states, sbs, vnbs, outs = {}, {}, {}, {}
        for hd in range(GDN_HEADS):
            qs, _, _ = head_slices(hd)
            states[hd] = state_ref[hd]
            sbs[hd] = states[hd].astype(BF16)
            v_new = u_scr[pl.ds(r0, CHUNK), qs] - _dot(w_scr[pl.ds(r0, CHUNK), qs], sbs[hd])
            vnbs[hd] = v_new.astype(BF16)
            vnew_scr[hd, pl.ds(in_sup, CHUNK), :] = vnbs[hd]
        for hd in range(GDN_HEADS):
            qs, _, _ = head_slices(hd)
            outs[hd] = (_dot(qd_scr[pl.ds(r0, CHUNK), qs], sbs[hd])
                        + _dot(attn_scr[hd, pl.ds(r0, CHUNK), :], vnew_scr[hd]))
        for hd in range(GDN_HEADS):
            qs, _, _ = head_slices(hd)
            gl = jnp.exp(g_end[:, G_LANE0 + hd:G_LANE0 + hd + 1])
            state_ref[hd] = states[hd] * gl + lax.dot_general(
                kd_scr[pl.ds(r0, CHUNK), qs], vnbs[hd], (((0,), (0,)), ((), ())),
                preferred_element_type=F32)
        for hd in range(GDN_HEADS):
            qs, _, _ = head_slices(hd)
            o = outs[hd]
            o = o * lax.rsqrt(jnp.mean(o * o, axis=-1, keepdims=True) + NORM_EPS) * normw
            zc = z_ref[pl.ds(r0, CHUNK), qs].astype(F32)
            y_ref[pl.ds(r0, CHUNK), qs] = (o * _silu(zc)).astype(y_ref.dtype)
        return 0

    lax.fori_loop(0, CHUNKS_PER_BLOCK, chunk_step, 0, unroll=2)


def _gdn_call(qkv, small, z, convw, normw):
    rows = qkv.shape[0]
    n_blocks = rows // ROW_BLOCK
    hist_blocks_per_row_block = ROW_BLOCK // SUBLANES
    return pl.pallas_call(
        _gdn_body,
        grid=(n_blocks,),
        in_specs=[
            pl.BlockSpec((ROW_BLOCK, 3 * GDN_K_WIDTH), lambda i: (i, 0)),
            pl.BlockSpec((SUBLANES, 3 * GDN_K_WIDTH),
                         lambda i: (jnp.maximum(i * hist_blocks_per_row_block - 1, 0), 0)),
            pl.BlockSpec((ROW_BLOCK, LANES), lambda i: (i, 0)),
            pl.BlockSpec((ROW_BLOCK, GDN_V_WIDTH), lambda i: (i, 0)),
            _resident(convw.shape), _resident(normw.shape),
        ],
        out_specs=pl.BlockSpec((ROW_BLOCK, GDN_V_WIDTH), lambda i: (i, 0)),
        out_shape=jax.ShapeDtypeStruct((rows, GDN_V_WIDTH), BF16),
        scratch_shapes=[pltpu.VMEM((GDN_HEADS, GDN_KEY_DIM, GDN_VAL_DIM), F32),
                        pltpu.VMEM((ROW_BLOCK, 3 * GDN_K_WIDTH), F32),
                        pltpu.VMEM((ROW_BLOCK, GDN_V_WIDTH), F32),
                        pltpu.VMEM((ROW_BLOCK, GDN_K_WIDTH), BF16),
                        pltpu.VMEM((GDN_HEADS, ROW_BLOCK, SUPER), BF16),
                        pltpu.VMEM((ROW_BLOCK, GDN_K_WIDTH), BF16),
                        pltpu.VMEM((ROW_BLOCK, GDN_K_WIDTH), BF16),
                        pltpu.VMEM((GDN_HEADS, SUPER, GDN_VAL_DIM), BF16)],
        compiler_params=pltpu.CompilerParams(dimension_semantics=("arbitrary",),
                                             vmem_limit_bytes=VMEM_LIMIT_BYTES),
        name="gated_delta_rule",
    )(qkv, qkv, small, z, convw, normw)


def _out_body(x_ref, ya_ref, yb_ref, ga_ref, gb_ref, lng_ref, lnb_ref, woa_ref, wob_ref, wo_ref,
              ln1g_ref, ln1b_ref, wgate_ref, wup_ref, wdown_ref, ln2g_ref, ln2b_ref, o_ref):
    h0 = _layer_norm(x_ref[...], lng_ref[...], lnb_ref[...])
    pa = _dot(ya_ref[...], woa_ref[...])
    pb = _dot(yb_ref[...], wob_ref[...])
    mixed = _sigmoid(ga_ref[...].astype(F32)) * pa + _sigmoid(gb_ref[...].astype(F32)) * pb
    h1 = _layer_norm(ALPHA * h0 + _dot(mixed.astype(BF16), wo_ref[...]), ln1g_ref[...], ln1b_ref[...])
    h1b = h1.astype(BF16)
    d_ff = wgate_ref.shape[1]
    half = d_ff // 2
    ffn = jnp.zeros_like(h1)
    for c in range(2):
        sl = slice(c * half, (c + 1) * half)
        gate = _dot(h1b, wgate_ref[:, sl])
        up = _dot(h1b, wup_ref[:, sl])
        ffn = ffn + _dot((_silu(gate) * up).astype(BF16), wdown_ref[sl, :])
    o_ref[...] = _layer_norm(ALPHA * h1 + ffn, ln2g_ref[...], ln2b_ref[...])


def _out_call(x, ya, yb, ga, gb, lng, lnb, woa, wob, wo, ln1g, ln1b, wgate, wup, wdown, ln2g, ln2b):
    seq = x.shape[0]
    n_blocks = seq // ROW_BLOCK
    main = lambda i: (i, 0)
    shifted = lambda i: (i + 1, 0)
    res = [lng, lnb, woa, wob, wo, ln1g, ln1b, wgate, wup, wdown, ln2g, ln2b]
    return pl.pallas_call(
        _out_body,
        grid=(n_blocks,),
        in_specs=[
            pl.BlockSpec((ROW_BLOCK, D_MODEL), main),
            pl.BlockSpec((ROW_BLOCK, ya.shape[1]), main),
            pl.BlockSpec((ROW_BLOCK, GDN_V_WIDTH), shifted),
            pl.BlockSpec((ROW_BLOCK, D_MODEL), shifted),
            pl.BlockSpec((ROW_BLOCK, D_MODEL), shifted),
        ] + [_resident(a.shape) for a in res],
        out_specs=pl.BlockSpec((ROW_BLOCK, D_MODEL), main),
        out_shape=jax.ShapeDtypeStruct((seq, D_MODEL), F32),
        compiler_params=pltpu.CompilerParams(dimension_semantics=("parallel",),
                                             vmem_limit_bytes=VMEM_LIMIT_BYTES),
        name="merge_out_ffn",
    )(x, ya, yb, ga, gb, *res)


def _slot_columns(w, scale=1.0):
    d = w.shape[0]
    w = (w * scale).reshape(d, FOX_HEADS, FOX_HEAD_DIM)
    w = jnp.pad(w, ((0, 0), (0, 0), (0, HEAD_SLOT - FOX_HEAD_DIM)))
    return w.reshape(d, FOX_HEADS * HEAD_SLOT)


def _slot_rows(w):
    d = w.shape[1]
    w = w.reshape(FOX_HEADS, FOX_HEAD_DIM, d)
    w = jnp.pad(w, ((0, 0), (0, HEAD_SLOT - FOX_HEAD_DIM), (0, 0)))
    return w.reshape(FOX_HEADS * HEAD_SLOT, d)


def _constants():
    wide = FOX_HEADS * HEAD_SLOT
    r = np.arange(ROW_BLOCK)
    tri_full = (r[:, None] >= r[None, :])
    tri_chunk = tri_full & ((r[:, None] // CHUNK) == (r[None, :] // CHUNK))
    tri = np.concatenate([tri_full, tri_chunk], axis=0).astype(np.float32)
    e_mat = np.zeros((3 * LANES, wide), np.float32)
    for hd in range(FOX_HEADS):
        for piece in range(3):
            e_mat[piece * LANES + LOGF_LANE0 + hd, hd * HEAD_SLOT + BIAS_LANE + piece] = 1.0
    return jnp.asarray(tri, BF16), jnp.asarray(e_mat, BF16)


def kernel(x, meta_tokens, ln_in_g, ln_in_b, w_in, b_f, conv_w, a_log, dt_bias, gdn_norm_w,
           w_out_a, w_out_b, w_o, ln1_g, ln1_b, w_gate, w_up, w_down, ln2_g, ln2_b):
    bsz, seq, d = x.shape
    assert bsz == 1 and d == D_MODEL and seq % Q_BLOCK == 0 and w_in.shape[0] == DEPTH
    x2 = x[0]
    w = w_in[0]
    o = 0
    cols = {}
    for name, width in (("q_a", FOX_WIDTH), ("k_a", FOX_WIDTH), ("v_a", FOX_WIDTH), ("f_a", FOX_HEADS),
                        ("qkv_b", 3 * GDN_K_WIDTH), ("a_b", GDN_HEADS), ("b_b", GDN_HEADS),
                        ("z_b", GDN_V_WIDTH), ("gate_a", D_MODEL), ("gate_b", D_MODEL)):
        cols[name] = w[:, o:o + width]
        o += width
    assert o == w.shape[1]

    wq = (cols["q_a"] * (FOX_HEAD_DIM ** -0.5 * LOG2E)).T.astype(BF16)
    wk = _slot_columns(cols["k_a"]).astype(BF16)
    wv = cols["v_a"].T.astype(BF16)
    wn = jnp.zeros((d, LANES), F32)
    wn = wn.at[:, LOGF_LANE0:LOGF_LANE0 + FOX_HEADS].set(cols["f_a"])
    wn = wn.at[:, G_LANE0:G_LANE0 + GDN_HEADS].set(cols["a_b"])
    wn = wn.at[:, BETA_LANE0:BETA_LANE0 + GDN_HEADS].set(cols["b_b"]).astype(BF16)
    par = jnp.zeros((SUBLANES, LANES), F32)
    par = par.at[0, LOGF_LANE0:LOGF_LANE0 + FOX_HEADS].set(b_f[0])
    par = par.at[1, G_LANE0:G_LANE0 + GDN_HEADS].set(a_log[0])
    par = par.at[2, G_LANE0:G_LANE0 + GDN_HEADS].set(dt_bias[0])
    tri, e_mat = _constants()
    prefix = jnp.zeros((PRE, d), F32).at[LEAD:].set(meta_tokens.astype(F32))
    row = lambda v: v.reshape(1, -1).astype(F32)

    qt, ka, vt, qkv, z, ga, gb, small, r_blk = _proj_call(
        prefix, x2, row(ln_in_g), row(ln_in_b), wq, wk, wv, cols["qkv_b"].astype(BF16),
        cols["z_b"].astype(BF16), cols["gate_a"].astype(BF16), cols["gate_b"].astype(BF16),
        wn, par, tri, e_mat)

    r_flat = r_blk[:, :SUBLANES].reshape(-1, SUBLANES, SUBLANES)[:, 0, :].reshape(-1)
    ya = _attn_call(r_flat, qt, ka, vt)
    yb = _gdn_call(qkv, small, z, conv_w[0].astype(F32), row(gdn_norm_w[0]))
    out = _out_call(x2, ya, yb, ga, gb, row(ln_in_g), row(ln_in_b),
                    _slot_rows(w_out_a[0]).astype(BF16), w_out_b[0].astype(BF16), w_o[0].astype(BF16),
                    row(ln1_g[0]), row(ln1_b[0]), w_gate[0].astype(BF16), w_up[0].astype(BF16),
                    w_down[0].astype(BF16), row(ln2_g[0]), row(ln2_b[0]))
    return out[None]
```

```python
import functools
import math

import jax
import jax.numpy as jnp
import numpy as np
from jax import lax
from jax.experimental import pallas as pl
from jax.experimental.pallas import tpu as pltpu

D_MODEL = 1024
CHUNK = 64
N_META = 16
FOX_HEADS = 8
FOX_HEAD_DIM = 64
FOX_WIDTH = FOX_HEADS * FOX_HEAD_DIM
GDN_HEADS = 8
GDN_KEY_DIM = 128
GDN_VAL_DIM = 128
GDN_K_WIDTH = GDN_HEADS * GDN_KEY_DIM
GDN_V_WIDTH = GDN_HEADS * GDN_VAL_DIM
CONV_WIDTH = 4
DEPTH = 1
ALPHA = (2.0 * DEPTH) ** 0.25
LN_EPS = 1e-5
NORM_EPS = 1e-6

LANES = 128
SUBLANES = 8
VMEM_LIMIT_BYTES = 56 * 1024 * 1024

ROW_BLOCK = 512
PRE = ROW_BLOCK
LEAD = PRE - N_META
CHUNKS_PER_BLOCK = ROW_BLOCK // CHUNK
Q_BLOCK = 2 * ROW_BLOCK
SUPER = 256
HEAD_GROUP = 4
HEAD_SLOT = LANES
BIAS_LANE = FOX_HEAD_DIM
V_ROWS = FOX_HEAD_DIM + 16
MASK_BIAS = -1e30
NEG_INIT = -1e30
LOG2E = math.log2(math.e)
F32 = jnp.float32
BF16 = jnp.bfloat16

LOGF_LANE0 = 0
G_LANE0 = 8
BETA_LANE0 = 16


def _nt_dot(a, b):
    return lax.dot_general(a, b, (((1,), (1,)), ((), ())), preferred_element_type=F32)


def _tn_nt_dot(a, b):
    return lax.dot_general(a, b, (((0,), (1,)), ((), ())), preferred_element_type=F32)


def _dot(a, b):
    return jnp.dot(a, b, preferred_element_type=F32)


def _layer_norm(x, g, b):
    mu = jnp.mean(x, axis=-1, keepdims=True)
    xc = x - mu
    var = jnp.mean(xc * xc, axis=-1, keepdims=True)
    return xc * lax.rsqrt(var + LN_EPS) * g + b


def _softplus(x):
    return jnp.maximum(x, 0.0) + jnp.log(1.0 + jnp.exp(-jnp.abs(x)))


def _sigmoid(x):
    return 1.0 / (1.0 + jnp.exp(-x))


def _silu(x):
    return x * _sigmoid(x)


def _split3(x):
    hi = x.astype(BF16)
    r1 = x - hi.astype(F32)
    mid = r1.astype(BF16)
    lo = (r1 - mid.astype(F32)).astype(BF16)
    return hi, mid, lo


def _resident(shape):
    nd = len(shape)
    return pl.BlockSpec(shape, lambda *_: (0,) * nd, pipeline_mode=pl.Buffered(1))


def _proj_body(prefix_ref, x_ref, lng_ref, lnb_ref, wq_ref, wk_ref, wv_ref, wg_ref, wz_ref, wga_ref,
               wgb_ref, wn_ref, par_ref, tri_ref, e_ref,
               qt_ref, ka_ref, vt_ref, qkv_ref, z_ref, ga_ref, gb_ref, small_ref, r_ref,
               carry_ref):
    i = pl.program_id(0)

    @pl.when(i == 0)
    def _():
        carry_ref[...] = jnp.zeros_like(carry_ref)

    src = jnp.where(i == 0, prefix_ref[...], x_ref[...])
    h = _layer_norm(src, lng_ref[...], lnb_ref[...])
    row = lax.broadcasted_iota(jnp.int32, (ROW_BLOCK, 1), 0)
    valid = jnp.logical_or(i > 0, row >= LEAD)
    hb = jnp.where(valid, h, 0.0).astype(BF16)

    nar = _dot(hb, wn_ref[...])
    lane = lax.broadcasted_iota(jnp.int32, (1, LANES), 1)
    b_f = par_ref[0:1, :]
    a_log = par_ref[1:2, :]
    dt_bias = par_ref[2:3, :]
    xf = nar + b_f
    logf = jnp.minimum(xf, 0.0) - jnp.log(1.0 + jnp.exp(-jnp.abs(xf)))
    g = -jnp.exp(a_log) * _softplus(nar + dt_bias)
    beta = _sigmoid(nar)
    is_logf = lane < G_LANE0
    is_g = jnp.logical_and(lane >= G_LANE0, lane < BETA_LANE0)
    vals = jnp.where(is_logf, logf, jnp.where(is_g, g, 0.0))

    hi, mid, lo = _split3(vals)
    tri = tri_ref[...]
    cs = _dot(tri, hi) + _dot(tri, mid) + _dot(tri, lo)
    d_blk = cs[:ROW_BLOCK]
    gc = cs[ROW_BLOCK:]
    carry = carry_ref[...]
    r_ref[...] = jnp.broadcast_to(carry, r_ref.shape)
    carry_ref[...] = carry + d_blk[ROW_BLOCK - 1:ROW_BLOCK, :]
    small_ref[...] = jnp.where(is_g, gc, jnp.where(lane >= BETA_LANE0, beta, d_blk))

    nd = jnp.where(valid, -LOG2E * d_blk, MASK_BIAS)
    nd = jnp.where(is_logf, nd, 0.0)
    phi, pmid, plo = _split3(nd)
    pieces = jnp.concatenate([phi, pmid, plo], axis=1)
    qt_ref[0] = _tn_nt_dot(wq_ref[...], hb).astype(BF16)
    ka_ref[...] = (_dot(hb, wk_ref[...]) + _dot(pieces, e_ref[...])).astype(BF16)
    vt_ref[0] = _tn_nt_dot(wv_ref[...], hb).astype(BF16)

    for c in range(3):
        sl = slice(c * GDN_K_WIDTH, (c + 1) * GDN_K_WIDTH)
        qkv_ref[:, sl] = _dot(hb, wg_ref[:, sl]).astype(BF16)
    z_ref[...] = _dot(hb, wz_ref[...]).astype(BF16)
    ga_ref[...] = _dot(hb, wga_ref[...]).astype(BF16)
    gb_ref[...] = _dot(hb, wgb_ref[...]).astype(BF16)


def _proj_call(prefix, x, lng, lnb, wq, wk, wv, wg, wz, wga, wgb, wn, par, tri, e_mat):
    seq = x.shape[0]
    n_blocks = seq // ROW_BLOCK + 1
    rows = n_blocks * ROW_BLOCK
    wide = FOX_HEADS * HEAD_SLOT

    def row_spec(width):
        return pl.BlockSpec((ROW_BLOCK, width), lambda i: (i, 0))

    in_specs = [
        _resident(prefix.shape),
        pl.BlockSpec((ROW_BLOCK, D_MODEL), lambda i: (jnp.maximum(i - 1, 0), 0)),
        _resident(lng.shape), _resident(lnb.shape),
        _resident(wq.shape), _resident(wk.shape), _resident(wv.shape), _resident(wg.shape),
        _resident(wz.shape), _resident(wga.shape), _resident(wgb.shape), _resident(wn.shape),
        _resident(par.shape), _resident(tri.shape), _resident(e_mat.shape),
    ]
    out_shape = [
        jax.ShapeDtypeStruct((n_blocks - 1, FOX_WIDTH, ROW_BLOCK), BF16),
        jax.ShapeDtypeStruct((rows, wide), BF16),
        jax.ShapeDtypeStruct((n_blocks, FOX_WIDTH, ROW_BLOCK), BF16),
        jax.ShapeDtypeStruct((rows, 3 * GDN_K_WIDTH), BF16),
        jax.ShapeDtypeStruct((rows, GDN_V_WIDTH), BF16),
        jax.ShapeDtypeStruct((rows, D_MODEL), BF16),
        jax.ShapeDtypeStruct((rows, D_MODEL), BF16),
        jax.ShapeDtypeStruct((rows, LANES), F32),
        jax.ShapeDtypeStruct((n_blocks * SUBLANES, LANES), F32),
    ]
    q_spec = pl.BlockSpec((1, FOX_WIDTH, ROW_BLOCK), lambda i: (jnp.maximum(i - 1, 0), 0, 0))
    v_spec = pl.BlockSpec((1, FOX_WIDTH, ROW_BLOCK), lambda i: (i, 0, 0))
    out_specs = [q_spec, row_spec(wide), v_spec, row_spec(3 * GDN_K_WIDTH),
                 row_spec(GDN_V_WIDTH), row_spec(D_MODEL), row_spec(D_MODEL), row_spec(LANES),
                 pl.BlockSpec((SUBLANES, LANES), lambda i: (i, 0))]
    return pl.pallas_call(
        _proj_body,
        grid=(n_blocks,),
        in_specs=in_specs,
        out_specs=out_specs,
        out_shape=out_shape,
        scratch_shapes=[pltpu.VMEM((1, LANES), F32)],
        compiler_params=pltpu.CompilerParams(dimension_semantics=("arbitrary",),
                                             vmem_limit_bytes=VMEM_LIMIT_BYTES),
        name="ln_in_proj",
    )(prefix, x, lng, lnb, wq, wk, wv, wg, wz, wga, wgb, wn, par, tri, e_mat)


def _attn_body(r_ref, qt_ref, k_ref, vt_ref, o_ref, s_scr, mx_scr, m_scr, acc_scr):
    h = pl.program_id(0)
    i = pl.program_id(1)
    q_t = jnp.concatenate([qt_ref[b] for b in range(Q_BLOCK // ROW_BLOCK)], axis=1)
    pad_row = lax.broadcasted_iota(jnp.int32, (HEAD_SLOT - FOX_HEAD_DIM, Q_BLOCK), 0)
    q_t = jnp.concatenate([q_t, jnp.where(pad_row < 3, 1.0, 0.0).astype(BF16)], axis=0)
    one_row = lax.broadcasted_iota(jnp.int32, (V_ROWS - FOX_HEAD_DIM, ROW_BLOCK), 0)
    v_ones = jnp.where(one_row == 0, 1.0, 0.0).astype(BF16)

    def produce(slot, j, mask=None):
        start = pl.multiple_of(j * ROW_BLOCK, ROW_BLOCK)
        s = _dot(k_ref[pl.ds(start, ROW_BLOCK), :], q_t)
        if mask is not None:
            s = jnp.where(mask, s, MASK_BIAS)
        s_scr[slot] = s
        mx_scr[slot] = jnp.max(s.reshape(ROW_BLOCK // SUBLANES, SUBLANES, Q_BLOCK), axis=0)

    def consume(slot, j):
        rj = r_ref[j * SUBLANES + h] * LOG2E
        m = m_scr[...]
        m_new = jnp.maximum(m, jnp.max(mx_scr[slot], axis=0, keepdims=True) - rj)
        p = jnp.exp2(s_scr[slot] - (m_new + rj))
        alpha = jnp.exp2(m - m_new)
        v_t = jnp.concatenate([vt_ref[j], v_ones], axis=0)
        acc_scr[...] = alpha * acc_scr[...] + _dot(v_t, p.astype(BF16))
        m_scr[...] = m_new

    m_scr[...] = jnp.full(m_scr.shape, NEG_INIT, F32)
    acc_scr[...] = jnp.zeros(acc_scr.shape, F32)
    produce(0, 0)

    def pair(j):
        produce(1, j + 1)
        consume(0, j)
        produce(0, j + 2)
        consume(1, j + 1)

    def quad(t, _):
        pair(4 * t)
        pair(4 * t + 2)
        return 0

    lax.fori_loop(0, i // 2, quad, 0)

    @pl.when(lax.rem(i, 2) == 1)
    def _():
        pair(2 * i - 2)

    first_diag = 2 * i + 1
    key = lax.broadcasted_iota(jnp.int32, (ROW_BLOCK, Q_BLOCK), 0)
    qry = lax.broadcasted_iota(jnp.int32, (ROW_BLOCK, Q_BLOCK), 1)
    produce(1, first_diag, qry >= key)
    consume(0, first_diag - 1)
    produce(0, first_diag + 1, qry >= key + ROW_BLOCK)
    consume(1, first_diag)
    consume(0, first_diag + 1)

    acc = acc_scr[...]
    denom = acc[FOX_HEAD_DIM:FOX_HEAD_DIM + 1, :]
    out_t = jnp.concatenate([acc[:FOX_HEAD_DIM] / denom,
                             jnp.zeros((HEAD_SLOT - FOX_HEAD_DIM, Q_BLOCK), F32)], axis=0)
    o_ref[...] = out_t.T.astype(o_ref.dtype)


def _attn_call(r_flat, qt, ka, vt):
    n_kv = vt.shape[0]
    n_q = qt.shape[0] * ROW_BLOCK // Q_BLOCK
    return pl.pallas_call(
        _attn_body,
        grid=(FOX_HEADS, n_q),
        in_specs=[
            pl.BlockSpec(memory_space=pltpu.SMEM),
            pl.BlockSpec((Q_BLOCK // ROW_BLOCK, FOX_HEAD_DIM, ROW_BLOCK), lambda h, i: (i, h, 0)),
            pl.BlockSpec((n_kv * ROW_BLOCK, HEAD_SLOT), lambda h, i: (0, h)),
            pl.BlockSpec((n_kv, FOX_HEAD_DIM, ROW_BLOCK), lambda h, i: (0, h, 0)),
        ],
        out_specs=pl.BlockSpec((Q_BLOCK, HEAD_SLOT), lambda h, i: (i, h)),
        out_shape=jax.ShapeDtypeStruct((n_q * Q_BLOCK, FOX_HEADS * HEAD_SLOT), BF16),
        scratch_shapes=[pltpu.VMEM((2, ROW_BLOCK, Q_BLOCK), F32),
                        pltpu.VMEM((2, SUBLANES, Q_BLOCK), F32),
                        pltpu.VMEM((1, Q_BLOCK), F32),
                        pltpu.VMEM((V_ROWS, Q_BLOCK), F32)],
        compiler_params=pltpu.CompilerParams(dimension_semantics=("arbitrary", "arbitrary"),
                                             vmem_limit_bytes=VMEM_LIMIT_BYTES),
        name="fox_attention",
    )(r_flat, qt, ka, vt)


def _gdn_body(qkv_ref, hist_ref, small_ref, z_ref, convw_ref, normw_ref, y_ref,
              state_ref, xn_ref, u_scr, w_scr, attn_scr, qd_scr, kd_scr, vnew_scr):
    i = pl.program_id(0)

    @pl.when(i == 0)
    def _():
        state_ref[...] = jnp.zeros_like(state_ref)
        vnew_scr[...] = jnp.zeros_like(vnew_scr)

    hist = jnp.where(i > 0, hist_ref[...].astype(F32), 0.0)
    cur = qkv_ref[...].astype(F32)
    ext = jnp.concatenate([hist, cur], axis=0)
    conv = convw_ref[CONV_WIDTH - 1:CONV_WIDTH, :] * cur
    for t in range(CONV_WIDTH - 1):
        off = SUBLANES - (CONV_WIDTH - 1) + t
        conv = conv + convw_ref[t:t + 1, :] * ext[off:off + ROW_BLOCK, :]
    act = _silu(conv)
    for hd in range(2 * GDN_HEADS):
        sl = slice(hd * GDN_KEY_DIM, (hd + 1) * GDN_KEY_DIM)
        blk = act[:, sl]
        scale = lax.rsqrt(jnp.sum(blk * blk, axis=-1, keepdims=True) + NORM_EPS)
        if hd < GDN_HEADS:
            scale = scale * (GDN_KEY_DIM ** -0.5)
        xn_ref[:, sl] = blk * scale
    xn_ref[:, 2 * GDN_K_WIDTH:] = act[:, 2 * GDN_K_WIDTH:]

    ri = lax.broadcasted_iota(jnp.int32, (SUPER, SUPER), 0)
    ci = lax.broadcasted_iota(jnp.int32, (SUPER, SUPER), 1)
    same = (ri // CHUNK) == (ci // CHUNK)
    incl = jnp.logical_and(same, ri >= ci)
    strict = jnp.logical_and(same, ri > ci)
    eye = (ri == ci).astype(F32)

    def head_slices(hd):
        qs = slice(hd * GDN_KEY_DIM, (hd + 1) * GDN_KEY_DIM)
        ks = slice(GDN_K_WIDTH + hd * GDN_KEY_DIM, GDN_K_WIDTH + (hd + 1) * GDN_KEY_DIM)
        vs = slice(2 * GDN_K_WIDTH + hd * GDN_VAL_DIM, 2 * GDN_K_WIDTH + (hd + 1) * GDN_VAL_DIM)
        return qs, ks, vs

    for sc in range(ROW_BLOCK // SUPER):
        rows = slice(sc * SUPER, (sc + 1) * SUPER)
        sm = small_ref[rows, :]
        sm_t = sm.T
        for h0 in range(0, GDN_HEADS, HEAD_GROUP):
            heads = range(h0, h0 + HEAD_GROUP)
            pw, pwb, t_inv, keep = {}, {}, {}, {}
            for hd in heads:
                qs, ks, vs = head_slices(hd)
                qc = xn_ref[rows, qs]
                kc = xn_ref[rows, ks]
                vc = xn_ref[rows, vs]
                gcol = sm[:, G_LANE0 + hd:G_LANE0 + hd + 1]
                bcol = sm[:, BETA_LANE0 + hd:BETA_LANE0 + hd + 1]
                grow = sm_t[G_LANE0 + hd:G_LANE0 + hd + 1, :]
                glast = jnp.concatenate(
                    [jnp.broadcast_to(gcol[(c + 1) * CHUNK - 1:(c + 1) * CHUNK, :], (CHUNK, 1))
                     for c in range(SUPER // CHUNK)], axis=0)
                decay = jnp.exp2(LOG2E * gcol - LOG2E * grow)
                egc = jnp.exp(gcol)
                kb = kc * bcol
                kcb = kc.astype(BF16)
                pw[hd] = jnp.where(strict, _nt_dot((-kb).astype(BF16), kcb) * decay, 0.0)
                pwb[hd] = pw[hd].astype(BF16)
                t_inv[hd] = eye + pw[hd]
                attn_scr[hd, rows, :] = jnp.where(
                    incl, _nt_dot(qc.astype(BF16), kcb) * decay, 0.0).astype(BF16)
                qd_scr[rows, qs] = (qc * egc).astype(BF16)
                kd_scr[rows, qs] = (kc * jnp.exp(glast - gcol)).astype(BF16)
                keep[hd] = jnp.concatenate([vc * bcol, kb * egc], axis=1).astype(BF16)
            for _k in range(int(math.log2(CHUNK)) - 1):
                for hd in heads:
                    pwb[hd] = _dot(pwb[hd], pwb[hd]).astype(BF16)
                for hd in heads:
                    t_inv[hd] = t_inv[hd] + _dot(t_inv[hd].astype(BF16), pwb[hd])
            for hd in heads:
                qs, _, _ = head_slices(hd)
                uw = _dot(t_inv[hd].astype(BF16), keep[hd])
                u_scr[rows, qs] = uw[:, :GDN_VAL_DIM]
                w_scr[rows, qs] = uw[:, GDN_VAL_DIM:].astype(BF16)

    normw = normw_ref[...]

    def chunk_step(c, _):
        r0 = pl.multiple_of(c * CHUNK, CHUNK)
        in_sup = pl.multiple_of(lax.rem(c, SUPER // CHUNK) * CHUNK, CHUNK)
        g_tail = small_ref[pl.ds(r0 + CHUNK - SUBLANES, SUBLANES), :]
        g_end = g_tail[SUBLANES - 1:SUBLANES, :]
        states, sbs, vnbs, outs = {}, {}, {}, {}
        for hd in range(GDN_HEADS):
            qs, _, _ = head_slices(hd)
            states[hd] = state_ref[hd]
            sbs[hd] = states[hd].astype(BF16)
            v_new = u_scr[pl.ds(r0, CHUNK), qs] - _dot(w_scr[pl.ds(r0, CHUNK), qs], sbs[hd])
            vnbs[hd] = v_new.astype(BF16)
            vnew_scr[hd, pl.ds(in_sup, CHUNK), :] = vnbs[hd]
        for hd in range(GDN_HEADS):
            qs, _, _ = head_slices(hd)
            outs[hd] = (_dot(qd_scr[pl.ds(r0, CHUNK), qs], sbs[hd])
                        + _dot(attn_scr[hd, pl.ds(r0, CHUNK), :], vnew_scr[hd]))
        for hd in range(GDN_HEADS):
            qs, _, _ = head_slices(hd)
            gl = jnp.exp(g_end[:, G_LANE0 + hd:G_LANE0 + hd + 1])
            state_ref[hd] = states[hd] * gl + lax.dot_general(
                kd_scr[pl.ds(r0, CHUNK), qs], vnbs[hd], (((0,), (0,)), ((), ())),
                preferred_element_type=F32)
        for hd in range(GDN_HEADS):
            qs, _, _ = head_slices(hd)
            o = outs[hd]
            o = o * lax.rsqrt(jnp.mean(o * o, axis=-1, keepdims=True) + NORM_EPS) * normw
            zc = z_ref[pl.ds(r0, CHUNK), qs].astype(F32)
            y_ref[pl.ds(r0, CHUNK), qs] = (o * _silu(zc)).astype(y_ref.dtype)
        return 0

    lax.fori_loop(0, CHUNKS_PER_BLOCK, chunk_step, 0, unroll=2)


def _gdn_call(qkv, small, z, convw, normw):
    rows = qkv.shape[0]
    n_blocks = rows // ROW_BLOCK
    hist_blocks_per_row_block = ROW_BLOCK // SUBLANES
    return pl.pallas_call(
        _gdn_body,
        grid=(n_blocks,),
        in_specs=[
            pl.BlockSpec((ROW_BLOCK, 3 * GDN_K_WIDTH), lambda i: (i, 0)),
            pl.BlockSpec((SUBLANES, 3 * GDN_K_WIDTH),
                         lambda i: (jnp.maximum(i * hist_blocks_per_row_block - 1, 0), 0)),
            pl.BlockSpec((ROW_BLOCK, LANES), lambda i: (i, 0)),
            pl.BlockSpec((ROW_BLOCK, GDN_V_WIDTH), lambda i: (i, 0)),
            _resident(convw.shape), _resident(normw.shape),
        ],
        out_specs=pl.BlockSpec((ROW_BLOCK, GDN_V_WIDTH), lambda i: (i, 0)),
        out_shape=jax.ShapeDtypeStruct((rows, GDN_V_WIDTH), BF16),
        scratch_shapes=[pltpu.VMEM((GDN_HEADS, GDN_KEY_DIM, GDN_VAL_DIM), F32),
                        pltpu.VMEM((ROW_BLOCK, 3 * GDN_K_WIDTH), F32),
                        pltpu.VMEM((ROW_BLOCK, GDN_V_WIDTH), F32),
                        pltpu.VMEM((ROW_BLOCK, GDN_K_WIDTH), BF16),
                        pltpu.VMEM((GDN_HEADS, ROW_BLOCK, SUPER), BF16),
                        pltpu.VMEM((ROW_BLOCK, GDN_K_WIDTH), BF16),
                        pltpu.VMEM((ROW_BLOCK, GDN_K_WIDTH), BF16),
                        pltpu.VMEM((GDN_HEADS, SUPER, GDN_VAL_DIM), BF16)],
        compiler_params=pltpu.CompilerParams(dimension_semantics=("arbitrary",),
                                             vmem_limit_bytes=VMEM_LIMIT_BYTES),
        name="gated_delta_rule",
    )(qkv, qkv, small, z, convw, normw)


def _out_body(x_ref, ya_ref, yb_ref, ga_ref, gb_ref, lng_ref, lnb_ref, woa_ref, wob_ref, wo_ref,
              ln1g_ref, ln1b_ref, wgate_ref, wup_ref, wdown_ref, ln2g_ref, ln2b_ref, o_ref):
    h0 = _layer_norm(x_ref[...], lng_ref[...], lnb_ref[...])
    pa = _dot(ya_ref[...], woa_ref[...])
    pb = _dot(yb_ref[...], wob_ref[...])
    mixed = _sigmoid(ga_ref[...].astype(F32)) * pa + _sigmoid(gb_ref[...].astype(F32)) * pb
    h1 = _layer_norm(ALPHA * h0 + _dot(mixed.astype(BF16), wo_ref[...]), ln1g_ref[...], ln1b_ref[...])
    h1b = h1.astype(BF16)
    d_ff = wgate_ref.shape[1]
    half = d_ff // 2
    ffn = jnp.zeros_like(h1)
    for c in range(2):
        sl = slice(c * half, (c + 1) * half)
        gate = _dot(h1b, wgate_ref[:, sl])
        up = _dot(h1b, wup_ref[:, sl])
        ffn = ffn + _dot((_silu(gate) * up).astype(BF16), wdown_ref[sl, :])
    o_ref[...] = _layer_norm(ALPHA * h1 + ffn, ln2g_ref[...], ln2b_ref[...])


def _out_call(x, ya, yb, ga, gb, lng, lnb, woa, wob, wo, ln1g, ln1b, wgate, wup, wdown, ln2g, ln2b):
    seq = x.shape[0]
    n_blocks = seq // ROW_BLOCK
    main = lambda i: (i, 0)
    shifted = lambda i: (i + 1, 0)
    res = [lng, lnb, woa, wob, wo, ln1g, ln1b, wgate, wup, wdown, ln2g, ln2b]
    return pl.pallas_call(
        _out_body,
        grid=(n_blocks,),
        in_specs=[
            pl.BlockSpec((ROW_BLOCK, D_MODEL), main),
            pl.BlockSpec((ROW_BLOCK, ya.shape[1]), main),
            pl.BlockSpec((ROW_BLOCK, GDN_V_WIDTH), shifted),
            pl.BlockSpec((ROW_BLOCK, D_MODEL), shifted),
            pl.BlockSpec((ROW_BLOCK, D_MODEL), shifted),
        ] + [_resident(a.shape) for a in res],
        out_specs=pl.BlockSpec((ROW_BLOCK, D_MODEL), main),
        out_shape=jax.ShapeDtypeStruct((seq, D_MODEL), F32),
        compiler_params=pltpu.CompilerParams(dimension_semantics=("parallel",),
                                             vmem_limit_bytes=VMEM_LIMIT_BYTES),
        name="merge_out_ffn",
    )(x, ya, yb, ga, gb, *res)


def _slot_columns(w, scale=1.0):
    d = w.shape[0]
    w = (w * scale).reshape(d, FOX_HEADS, FOX_HEAD_DIM)
    w = jnp.pad(w, ((0, 0), (0, 0), (0, HEAD_SLOT - FOX_HEAD_DIM)))
    return w.reshape(d, FOX_HEADS * HEAD_SLOT)


def _slot_rows(w):
    d = w.shape[1]
    w = w.reshape(FOX_HEADS, FOX_HEAD_DIM, d)
    w = jnp.pad(w, ((0, 0), (0, HEAD_SLOT - FOX_HEAD_DIM), (0, 0)))
    return w.reshape(FOX_HEADS * HEAD_SLOT, d)


def _constants():
    wide = FOX_HEADS * HEAD_SLOT
    r = np.arange(ROW_BLOCK)
    tri_full = (r[:, None] >= r[None, :])
    tri_chunk = tri_full & ((r[:, None] // CHUNK) == (r[None, :] // CHUNK))
    tri = np.concatenate([tri_full, tri_chunk], axis=0).astype(np.float32)
    e_mat = np.zeros((3 * LANES, wide), np.float32)
    for hd in range(FOX_HEADS):
        for piece in range(3):
            e_mat[piece * LANES + LOGF_LANE0 + hd, hd * HEAD_SLOT + BIAS_LANE + piece] = 1.0
    return jnp.asarray(tri, BF16), jnp.asarray(e_mat, BF16)


def kernel(x, meta_tokens, ln_in_g, ln_in_b, w_in, b_f, conv_w, a_log, dt_bias, gdn_norm_w,
           w_out_a, w_out_b, w_o, ln1_g, ln1_b, w_gate, w_up, w_down, ln2_g, ln2_b):
    bsz, seq, d = x.shape
    assert bsz == 1 and d == D_MODEL and seq % Q_BLOCK == 0 and w_in.shape[0] == DEPTH
    x2 = x[0]
    w = w_in[0]
    o = 0
    cols = {}
    for name, width in (("q_a", FOX_WIDTH), ("k_a", FOX_WIDTH), ("v_a", FOX_WIDTH), ("f_a", FOX_HEADS),
                        ("qkv_b", 3 * GDN_K_WIDTH), ("a_b", GDN_HEADS), ("b_b", GDN_HEADS),
                        ("z_b", GDN_V_WIDTH), ("gate_a", D_MODEL), ("gate_b", D_MODEL)):
        cols[name] = w[:, o:o + width]
        o += width
    assert o == w.shape[1]

    wq = (cols["q_a"] * (FOX_HEAD_DIM ** -0.5 * LOG2E)).astype(BF16)
    wk = _slot_columns(cols["k_a"]).astype(BF16)
    wv = cols["v_a"].astype(BF16)
    wn = jnp.zeros((d, LANES), F32)
    wn = wn.at[:, LOGF_LANE0:LOGF_LANE0 + FOX_HEADS].set(cols["f_a"])
    wn = wn.at[:, G_LANE0:G_LANE0 + GDN_HEADS].set(cols["a_b"])
    wn = wn.at[:, BETA_LANE0:BETA_LANE0 + GDN_HEADS].set(cols["b_b"]).astype(BF16)
    par = jnp.zeros((SUBLANES, LANES), F32)
    par = par.at[0, LOGF_LANE0:LOGF_LANE0 + FOX_HEADS].set(b_f[0])
    par = par.at[1, G_LANE0:G_LANE0 + GDN_HEADS].set(a_log[0])
    par = par.at[2, G_LANE0:G_LANE0 + GDN_HEADS].set(dt_bias[0])
    tri, e_mat = _constants()
    prefix = jnp.zeros((PRE, d), F32).at[LEAD:].set(meta_tokens.astype(F32))
    row = lambda v: v.reshape(1, -1).astype(F32)

    qt, ka, vt, qkv, z, ga, gb, small, r_blk = _proj_call(
        prefix, x2, row(ln_in_g), row(ln_in_b), wq, wk, wv, cols["qkv_b"].astype(BF16),
        cols["z_b"].astype(BF16), cols["gate_a"].astype(BF16), cols["gate_b"].astype(BF16),
        wn, par, tri, e_mat)

    r_flat = r_blk[:, :SUBLANES].reshape(-1, SUBLANES, SUBLANES)[:, 0, :].reshape(-1)
    ya = _attn_call(r_flat, qt, ka, vt)
    yb = _gdn_call(qkv, small, z, conv_w[0].astype(F32), row(gdn_norm_w[0]))
    out = _out_call(x2, ya, yb, ga, gb, row(ln_in_g), row(ln_in_b),
                    _slot_rows(w_out_a[0]).astype(BF16), w_out_b[0].astype(BF16), w_o[0].astype(BF16),
                    row(ln1_g[0]), row(ln1_b[0]), w_gate[0].astype(BF16), w_up[0].astype(BF16),
                    w_down[0].astype(BF16), row(ln2_g[0]), row(ln2_b[0]))
    return out[None]
```

```python
import functools
import math

import jax
import jax.numpy as jnp
import numpy as np
from jax import lax
from jax.experimental import pallas as pl
from jax.experimental.pallas import tpu as pltpu

D_MODEL = 1024
CHUNK = 64
N_META = 16
FOX_HEADS = 8
FOX_HEAD_DIM = 64
FOX_WIDTH = FOX_HEADS * FOX_HEAD_DIM
GDN_HEADS = 8
GDN_KEY_DIM = 128
GDN_VAL_DIM = 128
GDN_K_WIDTH = GDN_HEADS * GDN_KEY_DIM
GDN_V_WIDTH = GDN_HEADS * GDN_VAL_DIM
CONV_WIDTH = 4
DEPTH = 1
ALPHA = (2.0 * DEPTH) ** 0.25
LN_EPS = 1e-5
NORM_EPS = 1e-6

LANES = 128
SUBLANES = 8
VMEM_LIMIT_BYTES = 56 * 1024 * 1024

ROW_BLOCK = 512
PRE = ROW_BLOCK
LEAD = PRE - N_META
CHUNKS_PER_BLOCK = ROW_BLOCK // CHUNK
Q_BLOCK = 2 * ROW_BLOCK
SUPER = 128
HEAD_GROUP = 4
HEAD_SLOT = LANES
BIAS_LANE = FOX_HEAD_DIM
V_ROWS = FOX_HEAD_DIM + 16
MASK_BIAS = -1e30
NEG_INIT = -1e30
LOG2E = math.log2(math.e)
F32 = jnp.float32
BF16 = jnp.bfloat16

LOGF_LANE0 = 0
G_LANE0 = 8
BETA_LANE0 = 16


def _nt_dot(a, b):
    return lax.dot_general(a, b, (((1,), (1,)), ((), ())), preferred_element_type=F32)


def _dot(a, b):
    return jnp.dot(a, b, preferred_element_type=F32)


def _layer_norm(x, g, b):
    mu = jnp.mean(x, axis=-1, keepdims=True)
    xc = x - mu
    var = jnp.mean(xc * xc, axis=-1, keepdims=True)
    return xc * lax.rsqrt(var + LN_EPS) * g + b


def _softplus(x):
    return jnp.maximum(x, 0.0) + jnp.log(1.0 + jnp.exp(-jnp.abs(x)))


def _sigmoid(x):
    return 1.0 / (1.0 + jnp.exp(-x))


def _silu(x):
    return x * _sigmoid(x)


def _split3(x):
    hi = x.astype(BF16)
    r1 = x - hi.astype(F32)
    mid = r1.astype(BF16)
    lo = (r1 - mid.astype(F32)).astype(BF16)
    return hi, mid, lo


def _resident(shape):
    nd = len(shape)
    return pl.BlockSpec(shape, lambda *_: (0,) * nd, pipeline_mode=pl.Buffered(1))


def _proj_body(prefix_ref, x_ref, lng_ref, lnb_ref, wq_ref, wk_ref, wv_ref, wg_ref, wz_ref, wga_ref,
               wgb_ref, wn_ref, par_ref, tri_ref, e_ref,
               qt_ref, ka_ref, vt_ref, qkv_ref, z_ref, ga_ref, gb_ref, small_ref, r_ref,
               carry_ref):
    i = pl.program_id(0)

    @pl.when(i == 0)
    def _():
        carry_ref[...] = jnp.zeros_like(carry_ref)

    src = jnp.where(i == 0, prefix_ref[...], x_ref[...])
    h = _layer_norm(src, lng_ref[...], lnb_ref[...])
    row = lax.broadcasted_iota(jnp.int32, (ROW_BLOCK, 1), 0)
    valid = jnp.logical_or(i > 0, row >= LEAD)
    hb = jnp.where(valid, h, 0.0).astype(BF16)

    nar = _dot(hb, wn_ref[...])
    lane = lax.broadcasted_iota(jnp.int32, (1, LANES), 1)
    b_f = par_ref[0:1, :]
    a_log = par_ref[1:2, :]
    dt_bias = par_ref[2:3, :]
    xf = nar + b_f
    logf = jnp.minimum(xf, 0.0) - jnp.log(1.0 + jnp.exp(-jnp.abs(xf)))
    g = -jnp.exp(a_log) * _softplus(nar + dt_bias)
    beta = _sigmoid(nar)
    is_logf = lane < G_LANE0
    is_g = jnp.logical_and(lane >= G_LANE0, lane < BETA_LANE0)
    vals = jnp.where(is_logf, logf, jnp.where(is_g, g, 0.0))

    hi, mid, lo = _split3(vals)
    tri = tri_ref[...]
    cs = _dot(tri, hi) + _dot(tri, mid) + _dot(tri, lo)
    d_blk = cs[:ROW_BLOCK]
    gc = cs[ROW_BLOCK:]
    carry = carry_ref[...]
    r_ref[...] = jnp.broadcast_to(carry, r_ref.shape)
    carry_ref[...] = carry + d_blk[ROW_BLOCK - 1:ROW_BLOCK, :]
    small_ref[...] = jnp.where(is_g, gc, jnp.where(lane >= BETA_LANE0, beta, d_blk))

    nd = jnp.where(valid, -LOG2E * d_blk, MASK_BIAS)
    nd = jnp.where(is_logf, nd, 0.0)
    phi, pmid, plo = _split3(nd)
    pieces = jnp.concatenate([phi, pmid, plo], axis=1)
    qt_ref[0] = _nt_dot(wq_ref[...], hb).astype(BF16)
    ka_ref[...] = (_dot(hb, wk_ref[...]) + _dot(pieces, e_ref[...])).astype(BF16)
    vt_ref[0] = _nt_dot(wv_ref[...], hb).astype(BF16)

    for c in range(3):
        sl = slice(c * GDN_K_WIDTH, (c + 1) * GDN_K_WIDTH)
        qkv_ref[:, sl] = _dot(hb, wg_ref[:, sl]).astype(BF16)
    z_ref[...] = _dot(hb, wz_ref[...]).astype(BF16)
    ga_ref[...] = _dot(hb, wga_ref[...]).astype(BF16)
    gb_ref[...] = _dot(hb, wgb_ref[...]).astype(BF16)


def _proj_call(prefix, x, lng, lnb, wq, wk, wv, wg, wz, wga, wgb, wn, par, tri, e_mat):
    seq = x.shape[0]
    n_blocks = seq // ROW_BLOCK + 1
    rows = n_blocks * ROW_BLOCK
    wide = FOX_HEADS * HEAD_SLOT

    def row_spec(width):
        return pl.BlockSpec((ROW_BLOCK, width), lambda i: (i, 0))

    in_specs = [
        _resident(prefix.shape),
        pl.BlockSpec((ROW_BLOCK, D_MODEL), lambda i: (jnp.maximum(i - 1, 0), 0)),
        _resident(lng.shape), _resident(lnb.shape),
        _resident(wq.shape), _resident(wk.shape), _resident(wv.shape), _resident(wg.shape),
        _resident(wz.shape), _resident(wga.shape), _resident(wgb.shape), _resident(wn.shape),
        _resident(par.shape), _resident(tri.shape), _resident(e_mat.shape),
    ]
    out_shape = [
        jax.ShapeDtypeStruct((n_blocks - 1, FOX_WIDTH, ROW_BLOCK), BF16),
        jax.ShapeDtypeStruct((rows, wide), BF16),
        jax.ShapeDtypeStruct((n_blocks, FOX_WIDTH, ROW_BLOCK), BF16),
        jax.ShapeDtypeStruct((rows, 3 * GDN_K_WIDTH), BF16),
        jax.ShapeDtypeStruct((rows, GDN_V_WIDTH), BF16),
        jax.ShapeDtypeStruct((rows, D_MODEL), BF16),
        jax.ShapeDtypeStruct((rows, D_MODEL), BF16),
        jax.ShapeDtypeStruct((rows, LANES), F32),
        jax.ShapeDtypeStruct((n_blocks * SUBLANES, LANES), F32),
    ]
    q_spec = pl.BlockSpec((1, FOX_WIDTH, ROW_BLOCK), lambda i: (jnp.maximum(i - 1, 0), 0, 0))
    v_spec = pl.BlockSpec((1, FOX_WIDTH, ROW_BLOCK), lambda i: (i, 0, 0))
    out_specs = [q_spec, row_spec(wide), v_spec, row_spec(3 * GDN_K_WIDTH),
                 row_spec(GDN_V_WIDTH), row_spec(D_MODEL), row_spec(D_MODEL), row_spec(LANES),
                 pl.BlockSpec((SUBLANES, LANES), lambda i: (i, 0))]
    return pl.pallas_call(
        _proj_body,
        grid=(n_blocks,),
        in_specs=in_specs,
        out_specs=out_specs,
        out_shape=out_shape,
        scratch_shapes=[pltpu.VMEM((1, LANES), F32)],
        compiler_params=pltpu.CompilerParams(dimension_semantics=("arbitrary",),
                                             vmem_limit_bytes=VMEM_LIMIT_BYTES),
        name="ln_in_proj",
    )(prefix, x, lng, lnb, wq, wk, wv, wg, wz, wga, wgb, wn, par, tri, e_mat)


def _attn_body(r_ref, qt_ref, k_ref, vt_ref, o_ref, s_scr, mx_scr, m_scr, acc_scr):
    h = pl.program_id(0)
    i = pl.program_id(1)
    q_t = jnp.concatenate([qt_ref[b] for b in range(Q_BLOCK // ROW_BLOCK)], axis=1)
    pad_row = lax.broadcasted_iota(jnp.int32, (HEAD_SLOT - FOX_HEAD_DIM, Q_BLOCK), 0)
    q_t = jnp.concatenate([q_t, jnp.where(pad_row < 3, 1.0, 0.0).astype(BF16)], axis=0)
    one_row = lax.broadcasted_iota(jnp.int32, (V_ROWS - FOX_HEAD_DIM, ROW_BLOCK), 0)
    v_ones = jnp.where(one_row == 0, 1.0, 0.0).astype(BF16)

    def produce(slot, j, mask=None):
        start = pl.multiple_of(j * ROW_BLOCK, ROW_BLOCK)
        s = _dot(k_ref[pl.ds(start, ROW_BLOCK), :], q_t)
        if mask is not None:
            s = jnp.where(mask, s, MASK_BIAS)
        s_scr[slot] = s
        mx_scr[slot] = jnp.max(s.reshape(ROW_BLOCK // SUBLANES, SUBLANES, Q_BLOCK), axis=0)

    def consume(slot, j):
        rj = r_ref[j * SUBLANES + h] * LOG2E
        m = m_scr[...]
        m_new = jnp.maximum(m, jnp.max(mx_scr[slot], axis=0, keepdims=True) - rj)
        p = jnp.exp2(s_scr[slot] - (m_new + rj))
        alpha = jnp.exp2(m - m_new)
        v_t = jnp.concatenate([vt_ref[j], v_ones], axis=0)
        acc_scr[...] = alpha * acc_scr[...] + _dot(v_t, p.astype(BF16))
        m_scr[...] = m_new

    m_scr[...] = jnp.full(m_scr.shape, NEG_INIT, F32)
    acc_scr[...] = jnp.zeros(acc_scr.shape, F32)
    produce(0, 0)

    def pair(j):
        produce(1, j + 1)
        consume(0, j)
        produce(0, j + 2)
        consume(1, j + 1)

    def quad(t, _):
        pair(4 * t)
        pair(4 * t + 2)
        return 0

    lax.fori_loop(0, i // 2, quad, 0)

    @pl.when(lax.rem(i, 2) == 1)
    def _():
        pair(2 * i - 2)

    first_diag = 2 * i + 1
    key = lax.broadcasted_iota(jnp.int32, (ROW_BLOCK, Q_BLOCK), 0)
    qry = lax.broadcasted_iota(jnp.int32, (ROW_BLOCK, Q_BLOCK), 1)
    produce(1, first_diag, qry >= key)
    consume(0, first_diag - 1)
    produce(0, first_diag + 1, qry >= key + ROW_BLOCK)
    consume(1, first_diag)
    consume(0, first_diag + 1)

    acc = acc_scr[...]
    denom = acc[FOX_HEAD_DIM:FOX_HEAD_DIM + 1, :]
    out_t = jnp.concatenate([acc[:FOX_HEAD_DIM] / denom,
                             jnp.zeros((HEAD_SLOT - FOX_HEAD_DIM, Q_BLOCK), F32)], axis=0)
    o_ref[...] = out_t.T.astype(o_ref.dtype)


def _attn_call(r_flat, qt, ka, vt):
    n_kv = vt.shape[0]
    n_q = qt.shape[0] * ROW_BLOCK // Q_BLOCK
    return pl.pallas_call(
        _attn_body,
        grid=(FOX_HEADS, n_q),
        in_specs=[
            pl.BlockSpec(memory_space=pltpu.SMEM),
            pl.BlockSpec((Q_BLOCK // ROW_BLOCK, FOX_HEAD_DIM, ROW_BLOCK), lambda h, i: (i, h, 0)),
            pl.BlockSpec((n_kv * ROW_BLOCK, HEAD_SLOT), lambda h, i: (0, h)),
            pl.BlockSpec((n_kv, FOX_HEAD_DIM, ROW_BLOCK), lambda h, i: (0, h, 0)),
        ],
        out_specs=pl.BlockSpec((Q_BLOCK, HEAD_SLOT), lambda h, i: (i, h)),
        out_shape=jax.ShapeDtypeStruct((n_q * Q_BLOCK, FOX_HEADS * HEAD_SLOT), BF16),
        scratch_shapes=[pltpu.VMEM((2, ROW_BLOCK, Q_BLOCK), F32),
                        pltpu.VMEM((2, SUBLANES, Q_BLOCK), F32),
                        pltpu.VMEM((1, Q_BLOCK), F32),
                        pltpu.VMEM((V_ROWS, Q_BLOCK), F32)],
        compiler_params=pltpu.CompilerParams(dimension_semantics=("arbitrary", "arbitrary"),
                                             vmem_limit_bytes=VMEM_LIMIT_BYTES),
        name="fox_attention",
    )(r_flat, qt, ka, vt)


def _gdn_body(qkv_ref, hist_ref, small_ref, z_ref, convw_ref, normw_ref, y_ref,
              state_ref, xn_ref, u_scr, w_scr, attn_scr, qd_scr, kd_scr, vnew_scr):
    i = pl.program_id(0)

    @pl.when(i == 0)
    def _():
        state_ref[...] = jnp.zeros_like(state_ref)
        vnew_scr[...] = jnp.zeros_like(vnew_scr)

    hist = jnp.where(i > 0, hist_ref[...].astype(F32), 0.0)
    cur = qkv_ref[...].astype(F32)
    ext = jnp.concatenate([hist, cur], axis=0)
    conv = convw_ref[CONV_WIDTH - 1:CONV_WIDTH, :] * cur
    for t in range(CONV_WIDTH - 1):
        delay = CONV_WIDTH - 1 - t
        shifted = pltpu.roll(ext, delay, axis=0)[SUBLANES:, :]
        conv = conv + convw_ref[t:t + 1, :] * shifted
    act = _silu(conv)
    for hd in range(2 * GDN_HEADS):
        sl = slice(hd * GDN_KEY_DIM, (hd + 1) * GDN_KEY_DIM)
        blk = act[:, sl]
        scale = lax.rsqrt(jnp.sum(blk * blk, axis=-1, keepdims=True) + NORM_EPS)
        if hd < GDN_HEADS:
            scale = scale * (GDN_KEY_DIM ** -0.5)
        xn_ref[:, sl] = blk * scale
    xn_ref[:, 2 * GDN_K_WIDTH:] = act[:, 2 * GDN_K_WIDTH:]

    ri = lax.broadcasted_iota(jnp.int32, (SUPER, SUPER), 0)
    ci = lax.broadcasted_iota(jnp.int32, (SUPER, SUPER), 1)
    same = (ri // CHUNK) == (ci // CHUNK)
    incl = jnp.logical_and(same, ri >= ci)
    strict = jnp.logical_and(same, ri > ci)
    eye = (ri == ci).astype(F32)

    def head_slices(hd):
        qs = slice(hd * GDN_KEY_DIM, (hd + 1) * GDN_KEY_DIM)
        ks = slice(GDN_K_WIDTH + hd * GDN_KEY_DIM, GDN_K_WIDTH + (hd + 1) * GDN_KEY_DIM)
        vs = slice(2 * GDN_K_WIDTH + hd * GDN_VAL_DIM, 2 * GDN_K_WIDTH + (hd + 1) * GDN_VAL_DIM)
        return qs, ks, vs

    for sc in range(ROW_BLOCK // SUPER):
        rows = slice(sc * SUPER, (sc + 1) * SUPER)
        sm = small_ref[rows, :]
        sm_t = sm.T
        for h0 in range(0, GDN_HEADS, HEAD_GROUP):
            heads = range(h0, h0 + HEAD_GROUP)
            pw, pwb, t_inv, keep = {}, {}, {}, {}
            for hd in heads:
                qs, ks, vs = head_slices(hd)
                qc = xn_ref[rows, qs]
                kc = xn_ref[rows, ks]
                vc = xn_ref[rows, vs]
                gcol = sm[:, G_LANE0 + hd:G_LANE0 + hd + 1]
                bcol = sm[:, BETA_LANE0 + hd:BETA_LANE0 + hd + 1]
                grow = sm_t[G_LANE0 + hd:G_LANE0 + hd + 1, :]
                glast = jnp.concatenate(
                    [jnp.broadcast_to(gcol[(c + 1) * CHUNK - 1:(c + 1) * CHUNK, :], (CHUNK, 1))
                     for c in range(SUPER // CHUNK)], axis=0)
                decay = jnp.exp2(LOG2E * gcol - LOG2E * grow)
                egc = jnp.exp(gcol)
                kb = kc * bcol
                kcb = kc.astype(BF16)
                pw[hd] = jnp.where(strict, _nt_dot((-kb).astype(BF16), kcb) * decay, 0.0)
                pwb[hd] = pw[hd].astype(BF16)
                t_inv[hd] = eye + pw[hd]
                attn_scr[hd, rows, :] = jnp.where(
                    incl, _nt_dot(qc.astype(BF16), kcb) * decay, 0.0).astype(BF16)
                qd_scr[rows, qs] = (qc * egc).astype(BF16)
                kd_scr[rows, qs] = (kc * jnp.exp(glast - gcol)).astype(BF16)
                keep[hd] = jnp.concatenate([vc * bcol, kb * egc], axis=1).astype(BF16)
            for _k in range(int(math.log2(CHUNK)) - 1):
                for hd in heads:
                    pwb[hd] = _dot(pwb[hd], pwb[hd]).astype(BF16)
                for hd in heads:
                    t_inv[hd] = t_inv[hd] + _dot(t_inv[hd].astype(BF16), pwb[hd])
            for hd in heads:
                qs, _, _ = head_slices(hd)
                uw = _dot(t_inv[hd].astype(BF16), keep[hd])
                u_scr[rows, qs] = uw[:, :GDN_VAL_DIM]
                w_scr[rows, qs] = uw[:, GDN_VAL_DIM:].astype(BF16)

    normw = normw_ref[...]

    def chunk_step(c, _):
        r0 = pl.multiple_of(c * CHUNK, CHUNK)
        in_sup = pl.multiple_of(lax.rem(c, SUPER // CHUNK) * CHUNK, CHUNK)
        g_tail = small_ref[pl.ds(r0 + CHUNK - SUBLANES, SUBLANES), :]
        g_end = g_tail[SUBLANES - 1:SUBLANES, :]
        states, sbs, vnbs, outs = {}, {}, {}, {}
        for hd in range(GDN_HEADS):
            qs, _, _ = head_slices(hd)
            states[hd] = state_ref[hd]
            sbs[hd] = states[hd].astype(BF16)
            v_new = u_scr[pl.ds(r0, CHUNK), qs] - _dot(w_scr[pl.ds(r0, CHUNK), qs], sbs[hd])
            vnbs[hd] = v_new.astype(BF16)
            vnew_scr[hd, pl.ds(in_sup, CHUNK), :] = vnbs[hd]
        for hd in range(GDN_HEADS):
            qs, _, _ = head_slices(hd)
            outs[hd] = (_dot(qd_scr[pl.ds(r0, CHUNK), qs], sbs[hd])
                        + _dot(attn_scr[hd, pl.ds(r0, CHUNK), :], vnew_scr[hd]))
        for hd in range(GDN_HEADS):
            qs, _, _ = head_slices(hd)
            gl = jnp.exp(g_end[:, G_LANE0 + hd:G_LANE0 + hd + 1])
            state_ref[hd] = states[hd] * gl + lax.dot_general(
                kd_scr[pl.ds(r0, CHUNK), qs], vnbs[hd], (((0,), (0,)), ((), ())),
                preferred_element_type=F32)
        for hd in range(GDN_HEADS):
            qs, _, _ = head_slices(hd)
            o = outs[hd]
            o = o * lax.rsqrt(jnp.mean(o * o, axis=-1, keepdims=True) + NORM_EPS) * normw
            zc = z_ref[pl.ds(r0, CHUNK), qs].astype(F32)
            y_ref[pl.ds(r0, CHUNK), qs] = (o * _silu(zc)).astype(y_ref.dtype)
        return 0

    lax.fori_loop(0, CHUNKS_PER_BLOCK, chunk_step, 0, unroll=2)


def _gdn_call(qkv, small, z, convw, normw):
    rows = qkv.shape[0]
    n_blocks = rows // ROW_BLOCK
    hist_blocks_per_row_block = ROW_BLOCK // SUBLANES
    return pl.pallas_call(
        _gdn_body,
        grid=(n_blocks,),
        in_specs=[
            pl.BlockSpec((ROW_BLOCK, 3 * GDN_K_WIDTH), lambda i: (i, 0)),
            pl.BlockSpec((SUBLANES, 3 * GDN_K_WIDTH),
                         lambda i: (jnp.maximum(i * hist_blocks_per_row_block - 1, 0), 0)),
            pl.BlockSpec((ROW_BLOCK, LANES), lambda i: (i, 0)),
            pl.BlockSpec((ROW_BLOCK, GDN_V_WIDTH), lambda i: (i, 0)),
            _resident(convw.shape), _resident(normw.shape),
        ],
        out_specs=pl.BlockSpec((ROW_BLOCK, GDN_V_WIDTH), lambda i: (i, 0)),
        out_shape=jax.ShapeDtypeStruct((rows, GDN_V_WIDTH), BF16),
        scratch_shapes=[pltpu.VMEM((GDN_HEADS, GDN_KEY_DIM, GDN_VAL_DIM), F32),
                        pltpu.VMEM((ROW_BLOCK, 3 * GDN_K_WIDTH), F32),
                        pltpu.VMEM((ROW_BLOCK, GDN_V_WIDTH), F32),
                        pltpu.VMEM((ROW_BLOCK, GDN_K_WIDTH), BF16),
                        pltpu.VMEM((GDN_HEADS, ROW_BLOCK, SUPER), BF16),
                        pltpu.VMEM((ROW_BLOCK, GDN_K_WIDTH), BF16),
                        pltpu.VMEM((ROW_BLOCK, GDN_K_WIDTH), BF16),
                        pltpu.VMEM((GDN_HEADS, SUPER, GDN_VAL_DIM), BF16)],
        compiler_params=pltpu.CompilerParams(dimension_semantics=("arbitrary",),
                                             vmem_limit_bytes=VMEM_LIMIT_BYTES),
        name="gated_delta_rule",
    )(qkv, qkv, small, z, convw, normw)


def _out_body(x_ref, ya_ref, yb_ref, ga_ref, gb_ref, lng_ref, lnb_ref, woa_ref, wob_ref, wo_ref,
              ln1g_ref, ln1b_ref, wgate_ref, wup_ref, wdown_ref, ln2g_ref, ln2b_ref, o_ref):
    h0 = _layer_norm(x_ref[...], lng_ref[...], lnb_ref[...])
    pa = _dot(ya_ref[...], woa_ref[...])
    pb = _dot(yb_ref[...], wob_ref[...])
    mixed = _sigmoid(ga_ref[...].astype(F32)) * pa + _sigmoid(gb_ref[...].astype(F32)) * pb
    h1 = _layer_norm(ALPHA * h0 + _dot(mixed.astype(BF16), wo_ref[...]), ln1g_ref[...], ln1b_ref[...])
    h1b = h1.astype(BF16)
    d_ff = wgate_ref.shape[1]
    half = d_ff // 2
    ffn = jnp.zeros_like(h1)
    for c in range(2):
        sl = slice(c * half, (c + 1) * half)
        gate = _dot(h1b, wgate_ref[:, sl])
        up = _dot(h1b, wup_ref[:, sl])
        ffn = ffn + _dot((_silu(gate) * up).astype(BF16), wdown_ref[sl, :])
    o_ref[...] = _layer_norm(ALPHA * h1 + ffn, ln2g_ref[...], ln2b_ref[...])


def _out_call(x, ya, yb, ga, gb, lng, lnb, woa, wob, wo, ln1g, ln1b, wgate, wup, wdown, ln2g, ln2b):
    seq = x.shape[0]
    n_blocks = seq // ROW_BLOCK
    main = lambda i: (i, 0)
    shifted = lambda i: (i + 1, 0)
    res = [lng, lnb, woa, wob, wo, ln1g, ln1b, wgate, wup, wdown, ln2g, ln2b]
    return pl.pallas_call(
        _out_body,
        grid=(n_blocks,),
        in_specs=[
            pl.BlockSpec((ROW_BLOCK, D_MODEL), main),
            pl.BlockSpec((ROW_BLOCK, ya.shape[1]), main),
            pl.BlockSpec((ROW_BLOCK, GDN_V_WIDTH), shifted),
            pl.BlockSpec((ROW_BLOCK, D_MODEL), shifted),
            pl.BlockSpec((ROW_BLOCK, D_MODEL), shifted),
        ] + [_resident(a.shape) for a in res],
        out_specs=pl.BlockSpec((ROW_BLOCK, D_MODEL), main),
        out_shape=jax.ShapeDtypeStruct((seq, D_MODEL), F32),
        compiler_params=pltpu.CompilerParams(dimension_semantics=("parallel",),
                                             vmem_limit_bytes=VMEM_LIMIT_BYTES),
        name="merge_out_ffn",
    )(x, ya, yb, ga, gb, *res)


def _slot_columns(w):
    d = w.shape[0]
    w = w.reshape(d, FOX_HEADS, FOX_HEAD_DIM)
    w = jnp.pad(w, ((0, 0), (0, 0), (0, HEAD_SLOT - FOX_HEAD_DIM)))
    return w.reshape(d, FOX_HEADS * HEAD_SLOT)


def _slot_rows(w):
    d = w.shape[1]
    w = w.reshape(FOX_HEADS, FOX_HEAD_DIM, d)
    w = jnp.pad(w, ((0, 0), (0, HEAD_SLOT - FOX_HEAD_DIM), (0, 0)))
    return w.reshape(FOX_HEADS * HEAD_SLOT, d)


def _constants():
    wide = FOX_HEADS * HEAD_SLOT
    r = np.arange(ROW_BLOCK)
    tri_full = (r[:, None] >= r[None, :])
    tri_chunk = tri_full & ((r[:, None] // CHUNK) == (r[None, :] // CHUNK))
    tri = np.concatenate([tri_full, tri_chunk], axis=0).astype(np.float32)
    e_mat = np.zeros((3 * LANES, wide), np.float32)
    for hd in range(FOX_HEADS):
        for piece in range(3):
            e_mat[piece * LANES + LOGF_LANE0 + hd, hd * HEAD_SLOT + BIAS_LANE + piece] = 1.0
    return jnp.asarray(tri, BF16), jnp.asarray(e_mat, BF16)


def kernel(x, meta_tokens, ln_in_g, ln_in_b, w_in, b_f, conv_w, a_log, dt_bias, gdn_norm_w,
           w_out_a, w_out_b, w_o, ln1_g, ln1_b, w_gate, w_up, w_down, ln2_g, ln2_b):
    bsz, seq, d = x.shape
    assert bsz == 1 and d == D_MODEL and seq % Q_BLOCK == 0 and w_in.shape[0] == DEPTH
    x2 = x[0]
    wt = jnp.swapaxes(w_in[0], 0, 1)
    o = 0
    part = {}
    for name, width in (("q_a", FOX_WIDTH), ("k_a", FOX_WIDTH), ("v_a", FOX_WIDTH), ("f_a", FOX_HEADS),
                        ("qkv_b", 3 * GDN_K_WIDTH), ("a_b", GDN_HEADS), ("b_b", GDN_HEADS),
                        ("z_b", GDN_V_WIDTH), ("gate_a", D_MODEL), ("gate_b", D_MODEL)):
        part[name] = wt[o:o + width]
        o += width
    assert o == wt.shape[0]
    cols = {name: part[name].astype(BF16).T for name in ("k_a", "qkv_b", "z_b", "gate_a", "gate_b")}

    wq = (part["q_a"] * (FOX_HEAD_DIM ** -0.5 * LOG2E)).astype(BF16)
    wk = _slot_columns(cols["k_a"])
    wv = part["v_a"].astype(BF16)
    wn = jnp.zeros((LANES, d), F32)
    wn = wn.at[LOGF_LANE0:LOGF_LANE0 + FOX_HEADS].set(part["f_a"])
    wn = wn.at[G_LANE0:G_LANE0 + GDN_HEADS].set(part["a_b"])
    wn = wn.at[BETA_LANE0:BETA_LANE0 + GDN_HEADS].set(part["b_b"]).astype(BF16).T
    par = jnp.zeros((SUBLANES, LANES), F32)
    par = par.at[0, LOGF_LANE0:LOGF_LANE0 + FOX_HEADS].set(b_f[0])
    par = par.at[1, G_LANE0:G_LANE0 + GDN_HEADS].set(a_log[0])
    par = par.at[2, G_LANE0:G_LANE0 + GDN_HEADS].set(dt_bias[0])
    tri, e_mat = _constants()
    prefix = jnp.zeros((PRE, d), F32).at[LEAD:].set(meta_tokens.astype(F32))
    row = lambda v: v.reshape(1, -1).astype(F32)

    qt, ka, vt, qkv, z, ga, gb, small, r_blk = _proj_call(
        prefix, x2, row(ln_in_g), row(ln_in_b), wq, wk, wv, cols["qkv_b"],
        cols["z_b"], cols["gate_a"], cols["gate_b"],
        wn, par, tri, e_mat)

    r_flat = r_blk[:, :SUBLANES].reshape(-1, SUBLANES, SUBLANES)[:, 0, :].reshape(-1)
    ya = _attn_call(r_flat, qt, ka, vt)
    yb = _gdn_call(qkv, small, z, conv_w[0].astype(F32), row(gdn_norm_w[0]))
    out = _out_call(x2, ya, yb, ga, gb, row(ln_in_g), row(ln_in_b),
                    _slot_rows(w_out_a[0]).astype(BF16), w_out_b[0].astype(BF16), w_o[0].astype(BF16),
                    row(ln1_g[0]), row(ln1_b[0]), w_gate[0].astype(BF16), w_up[0].astype(BF16),
                    w_down[0].astype(BF16), row(ln2_g[0]), row(ln2_b[0]))
    return out[None]
```

```python
import functools
import math

import jax
import jax.numpy as jnp
import numpy as np
from jax import lax
from jax.experimental import pallas as pl
from jax.experimental.pallas import tpu as pltpu

D_MODEL = 1024
CHUNK = 64
N_META = 16
FOX_HEADS = 8
FOX_HEAD_DIM = 64
FOX_WIDTH = FOX_HEADS * FOX_HEAD_DIM
GDN_HEADS = 8
GDN_KEY_DIM = 128
GDN_VAL_DIM = 128
GDN_K_WIDTH = GDN_HEADS * GDN_KEY_DIM
GDN_V_WIDTH = GDN_HEADS * GDN_VAL_DIM
CONV_WIDTH = 4
DEPTH = 1
ALPHA = (2.0 * DEPTH) ** 0.25
LN_EPS = 1e-5
NORM_EPS = 1e-6

LANES = 128
SUBLANES = 8
VMEM_LIMIT_BYTES = 56 * 1024 * 1024

ROW_BLOCK = 512
PRE = ROW_BLOCK
LEAD = PRE - N_META
CHUNKS_PER_BLOCK = ROW_BLOCK // CHUNK
Q_BLOCK = 2 * ROW_BLOCK
HIST_ROWS = 16
SHIFT_TILE = 128
SUPER = 128
HEAD_GROUP = 4
HEAD_SLOT = LANES
BIAS_LANE = FOX_HEAD_DIM
V_ROWS = FOX_HEAD_DIM + 16
MASK_BIAS = -1e30
NEG_INIT = -1e30
LOG2E = math.log2(math.e)
F32 = jnp.float32
BF16 = jnp.bfloat16

LOGF_LANE0 = 0
G_LANE0 = 8
BETA_LANE0 = 16


def _nt_dot(a, b):
    return lax.dot_general(a, b, (((1,), (1,)), ((), ())), preferred_element_type=F32)


def _dot(a, b):
    return jnp.dot(a, b, preferred_element_type=F32)


def _layer_norm(x, g, b):
    mu = jnp.mean(x, axis=-1, keepdims=True)
    xc = x - mu
    var = jnp.mean(xc * xc, axis=-1, keepdims=True)
    return xc * lax.rsqrt(var + LN_EPS) * g + b


def _softplus(x):
    return jnp.maximum(x, 0.0) + jnp.log(1.0 + jnp.exp(-jnp.abs(x)))


def _sigmoid(x):
    return 1.0 / (1.0 + jnp.exp(-x))


def _silu(x):
    return x * _sigmoid(x)


def _split3(x):
    hi = x.astype(BF16)
    r1 = x - hi.astype(F32)
    mid = r1.astype(BF16)
    lo = (r1 - mid.astype(F32)).astype(BF16)
    return hi, mid, lo


def _resident(shape):
    nd = len(shape)
    return pl.BlockSpec(shape, lambda *_: (0,) * nd, pipeline_mode=pl.Buffered(1))


def _proj_body(prefix_ref, x_ref, lng_ref, lnb_ref, wq_ref, wk_ref, wv_ref, wg_ref, wz_ref, wga_ref,
               wgb_ref, wn_ref, par_ref, tri_ref, e_ref,
               qt_ref, ka_ref, vt_ref, qkv_ref, z_ref, ga_ref, gb_ref, small_ref, r_ref,
               carry_ref):
    i = pl.program_id(0)

    @pl.when(i == 0)
    def _():
        carry_ref[...] = jnp.zeros_like(carry_ref)

    src = jnp.where(i == 0, prefix_ref[...], x_ref[...])
    h = _layer_norm(src, lng_ref[...], lnb_ref[...])
    row = lax.broadcasted_iota(jnp.int32, (ROW_BLOCK, 1), 0)
    valid = jnp.logical_or(i > 0, row >= LEAD)
    hb = jnp.where(valid, h, 0.0).astype(BF16)

    nar = _dot(hb, wn_ref[...])
    lane = lax.broadcasted_iota(jnp.int32, (1, LANES), 1)
    b_f = par_ref[0:1, :]
    a_log = par_ref[1:2, :]
    dt_bias = par_ref[2:3, :]
    xf = nar + b_f
    logf = jnp.minimum(xf, 0.0) - jnp.log(1.0 + jnp.exp(-jnp.abs(xf)))
    g = -jnp.exp(a_log) * _softplus(nar + dt_bias)
    beta = _sigmoid(nar)
    is_logf = lane < G_LANE0
    is_g = jnp.logical_and(lane >= G_LANE0, lane < BETA_LANE0)
    vals = jnp.where(is_logf, logf, jnp.where(is_g, g, 0.0))

    hi, mid, lo = _split3(vals)
    tri = tri_ref[...]
    cs = _dot(tri, hi) + _dot(tri, mid) + _dot(tri, lo)
    d_blk = cs[:ROW_BLOCK]
    gc = cs[ROW_BLOCK:]
    carry = carry_ref[...]
    r_ref[...] = jnp.broadcast_to(carry, r_ref.shape)
    carry_ref[...] = carry + d_blk[ROW_BLOCK - 1:ROW_BLOCK, :]
    small_ref[...] = jnp.where(is_g, gc, jnp.where(lane >= BETA_LANE0, beta, d_blk))

    nd = jnp.where(valid, -LOG2E * d_blk, MASK_BIAS)
    nd = jnp.where(is_logf, nd, 0.0)
    phi, pmid, plo = _split3(nd)
    pieces = jnp.concatenate([phi, pmid, plo], axis=1)
    qt_ref[0] = _nt_dot(wq_ref[...], hb).astype(BF16)
    ka_ref[...] = (_dot(hb, wk_ref[...]) + _dot(pieces, e_ref[...])).astype(BF16)
    vt_ref[0] = _nt_dot(wv_ref[...], hb).astype(BF16)

    for c in range(3):
        sl = slice(c * GDN_K_WIDTH, (c + 1) * GDN_K_WIDTH)
        qkv_ref[:, sl] = _dot(hb, wg_ref[:, sl]).astype(BF16)
    z_ref[...] = _dot(hb, wz_ref[...]).astype(BF16)
    ga_ref[...] = _dot(hb, wga_ref[...]).astype(BF16)
    gb_ref[...] = _dot(hb, wgb_ref[...]).astype(BF16)


def _proj_call(prefix, x, lng, lnb, wq, wk, wv, wg, wz, wga, wgb, wn, par, tri, e_mat):
    seq = x.shape[0]
    n_blocks = seq // ROW_BLOCK + 1
    rows = n_blocks * ROW_BLOCK
    wide = FOX_HEADS * HEAD_SLOT

    def row_spec(width):
        return pl.BlockSpec((ROW_BLOCK, width), lambda i: (i, 0))

    in_specs = [
        _resident(prefix.shape),
        pl.BlockSpec((ROW_BLOCK, D_MODEL), lambda i: (jnp.maximum(i - 1, 0), 0)),
        _resident(lng.shape), _resident(lnb.shape),
        _resident(wq.shape), _resident(wk.shape), _resident(wv.shape), _resident(wg.shape),
        _resident(wz.shape), _resident(wga.shape), _resident(wgb.shape), _resident(wn.shape),
        _resident(par.shape), _resident(tri.shape), _resident(e_mat.shape),
    ]
    out_shape = [
        jax.ShapeDtypeStruct((n_blocks - 1, FOX_WIDTH, ROW_BLOCK), BF16),
        jax.ShapeDtypeStruct((rows, wide), BF16),
        jax.ShapeDtypeStruct((n_blocks, FOX_WIDTH, ROW_BLOCK), BF16),
        jax.ShapeDtypeStruct((rows, 3 * GDN_K_WIDTH), BF16),
        jax.ShapeDtypeStruct((rows, GDN_V_WIDTH), BF16),
        jax.ShapeDtypeStruct((rows, D_MODEL), BF16),
        jax.ShapeDtypeStruct((rows, D_MODEL), BF16),
        jax.ShapeDtypeStruct((rows, LANES), F32),
        jax.ShapeDtypeStruct((n_blocks * SUBLANES, LANES), F32),
    ]
    q_spec = pl.BlockSpec((1, FOX_WIDTH, ROW_BLOCK), lambda i: (jnp.maximum(i - 1, 0), 0, 0))
    v_spec = pl.BlockSpec((1, FOX_WIDTH, ROW_BLOCK), lambda i: (i, 0, 0))
    out_specs = [q_spec, row_spec(wide), v_spec, row_spec(3 * GDN_K_WIDTH),
                 row_spec(GDN_V_WIDTH), row_spec(D_MODEL), row_spec(D_MODEL), row_spec(LANES),
                 pl.BlockSpec((SUBLANES, LANES), lambda i: (i, 0))]
    return pl.pallas_call(
        _proj_body,
        grid=(n_blocks,),
        in_specs=in_specs,
        out_specs=out_specs,
        out_shape=out_shape,
        scratch_shapes=[pltpu.VMEM((1, LANES), F32)],
        compiler_params=pltpu.CompilerParams(dimension_semantics=("arbitrary",),
                                             vmem_limit_bytes=VMEM_LIMIT_BYTES),
        name="ln_in_proj",
    )(prefix, x, lng, lnb, wq, wk, wv, wg, wz, wga, wgb, wn, par, tri, e_mat)


def _attn_body(r_ref, qt_ref, k_ref, vt_ref, o_ref, s_scr, mx_scr, m_scr, acc_scr):
    h = pl.program_id(0)
    i = pl.program_id(1)
    q_t = jnp.concatenate([qt_ref[b] for b in range(Q_BLOCK // ROW_BLOCK)], axis=1)
    pad_row = lax.broadcasted_iota(jnp.int32, (HEAD_SLOT - FOX_HEAD_DIM, Q_BLOCK), 0)
    q_t = jnp.concatenate([q_t, jnp.where(pad_row < 3, 1.0, 0.0).astype(BF16)], axis=0)
    one_row = lax.broadcasted_iota(jnp.int32, (V_ROWS - FOX_HEAD_DIM, ROW_BLOCK), 0)
    v_ones = jnp.where(one_row == 0, 1.0, 0.0).astype(BF16)

    def produce(slot, j, mask=None):
        start = pl.multiple_of(j * ROW_BLOCK, ROW_BLOCK)
        s = _dot(k_ref[pl.ds(start, ROW_BLOCK), :], q_t)
        if mask is not None:
            s = jnp.where(mask, s, MASK_BIAS)
        s_scr[slot] = s
        mx_scr[slot] = jnp.max(s.reshape(ROW_BLOCK // SUBLANES, SUBLANES, Q_BLOCK), axis=0)

    def consume(slot, j):
        rj = r_ref[j * SUBLANES + h] * LOG2E
        m = m_scr[...]
        m_new = jnp.maximum(m, jnp.max(mx_scr[slot], axis=0, keepdims=True) - rj)
        p = jnp.exp2(s_scr[slot] - (m_new + rj))
        alpha = jnp.exp2(m - m_new)
        v_t = jnp.concatenate([vt_ref[j], v_ones], axis=0)
        acc_scr[...] = alpha * acc_scr[...] + _dot(v_t, p.astype(BF16))
        m_scr[...] = m_new

    m_scr[...] = jnp.full(m_scr.shape, NEG_INIT, F32)
    acc_scr[...] = jnp.zeros(acc_scr.shape, F32)
    produce(0, 0)

    def pair(j):
        produce(1, j + 1)
        consume(0, j)
        produce(0, j + 2)
        consume(1, j + 1)

    def quad(t, _):
        pair(4 * t)
        pair(4 * t + 2)
        return 0

    lax.fori_loop(0, i // 2, quad, 0)

    @pl.when(lax.rem(i, 2) == 1)
    def _():
        pair(2 * i - 2)

    first_diag = 2 * i + 1
    key = lax.broadcasted_iota(jnp.int32, (ROW_BLOCK, Q_BLOCK), 0)
    qry = lax.broadcasted_iota(jnp.int32, (ROW_BLOCK, Q_BLOCK), 1)
    produce(1, first_diag, qry >= key)
    consume(0, first_diag - 1)
    produce(0, first_diag + 1, qry >= key + ROW_BLOCK)
    consume(1, first_diag)
    consume(0, first_diag + 1)

    acc = acc_scr[...]
    denom = acc[FOX_HEAD_DIM:FOX_HEAD_DIM + 1, :]
    out_t = jnp.concatenate([acc[:FOX_HEAD_DIM] / denom,
                             jnp.zeros((HEAD_SLOT - FOX_HEAD_DIM, Q_BLOCK), F32)], axis=0)
    o_ref[...] = out_t.T.astype(o_ref.dtype)


def _attn_call(r_flat, qt, ka, vt):
    n_kv = vt.shape[0]
    n_q = qt.shape[0] * ROW_BLOCK // Q_BLOCK
    return pl.pallas_call(
        _attn_body,
        grid=(FOX_HEADS, n_q),
        in_specs=[
            pl.BlockSpec(memory_space=pltpu.SMEM),
            pl.BlockSpec((Q_BLOCK // ROW_BLOCK, FOX_HEAD_DIM, ROW_BLOCK), lambda h, i: (i, h, 0)),
            pl.BlockSpec((n_kv * ROW_BLOCK, HEAD_SLOT), lambda h, i: (0, h)),
            pl.BlockSpec((n_kv, FOX_HEAD_DIM, ROW_BLOCK), lambda h, i: (0, h, 0)),
        ],
        out_specs=pl.BlockSpec((Q_BLOCK, HEAD_SLOT), lambda h, i: (i, h)),
        out_shape=jax.ShapeDtypeStruct((n_q * Q_BLOCK, FOX_HEADS * HEAD_SLOT), BF16),
        scratch_shapes=[pltpu.VMEM((2, ROW_BLOCK, Q_BLOCK), F32),
                        pltpu.VMEM((2, SUBLANES, Q_BLOCK), F32),
                        pltpu.VMEM((1, Q_BLOCK), F32),
                        pltpu.VMEM((V_ROWS, Q_BLOCK), F32)],
        compiler_params=pltpu.CompilerParams(dimension_semantics=("arbitrary", "arbitrary"),
                                             vmem_limit_bytes=VMEM_LIMIT_BYTES),
        name="fox_attention",
    )(r_flat, qt, ka, vt)


def _gdn_body(qkv_ref, hist_ref, small_ref, z_ref, convw_ref, shift_ref, normw_ref, y_ref,
              state_ref, xn_ref, u_scr, w_scr, attn_scr, qd_scr, kd_scr, vnew_scr):
    i = pl.program_id(0)

    @pl.when(i == 0)
    def _():
        state_ref[...] = jnp.zeros_like(state_ref)
        vnew_scr[...] = jnp.zeros_like(vnew_scr)

    hist = hist_ref[...]
    hist = jnp.where(i > 0, hist, jnp.zeros_like(hist))
    cur_b = qkv_ref[...]
    ext = jnp.concatenate([hist, cur_b], axis=0)
    n_tiles = ROW_BLOCK // SHIFT_TILE
    delayed_tiles = [_dot(shift_ref[...], ext[g * SHIFT_TILE:g * SHIFT_TILE + SHIFT_TILE + HIST_ROWS, :])
                     for g in range(n_tiles)]
    tiles = []
    for g in range(n_tiles):
        delayed = delayed_tiles[g]
        acc = convw_ref[CONV_WIDTH - 1:CONV_WIDTH, :] * cur_b[g * SHIFT_TILE:(g + 1) * SHIFT_TILE, :].astype(F32)
        for delay in range(1, CONV_WIDTH):
            t = CONV_WIDTH - 1 - delay
            acc = acc + convw_ref[t:t + 1, :] * delayed[(delay - 1) * SHIFT_TILE:delay * SHIFT_TILE, :]
        tiles.append(acc)
    act = _silu(jnp.concatenate(tiles, axis=0))
    for hd in range(2 * GDN_HEADS):
        sl = slice(hd * GDN_KEY_DIM, (hd + 1) * GDN_KEY_DIM)
        blk = act[:, sl]
        scale = lax.rsqrt(jnp.sum(blk * blk, axis=-1, keepdims=True) + NORM_EPS)
        if hd < GDN_HEADS:
            scale = scale * (GDN_KEY_DIM ** -0.5)
        xn_ref[:, sl] = blk * scale
    xn_ref[:, 2 * GDN_K_WIDTH:] = act[:, 2 * GDN_K_WIDTH:]

    ri = lax.broadcasted_iota(jnp.int32, (SUPER, SUPER), 0)
    ci = lax.broadcasted_iota(jnp.int32, (SUPER, SUPER), 1)
    same = (ri // CHUNK) == (ci // CHUNK)
    incl = jnp.logical_and(same, ri >= ci)
    strict = jnp.logical_and(same, ri > ci)
    eye = (ri == ci).astype(F32)

    def head_slices(hd):
        qs = slice(hd * GDN_KEY_DIM, (hd + 1) * GDN_KEY_DIM)
        ks = slice(GDN_K_WIDTH + hd * GDN_KEY_DIM, GDN_K_WIDTH + (hd + 1) * GDN_KEY_DIM)
        vs = slice(2 * GDN_K_WIDTH + hd * GDN_VAL_DIM, 2 * GDN_K_WIDTH + (hd + 1) * GDN_VAL_DIM)
        return qs, ks, vs

    for sc in range(ROW_BLOCK // SUPER):
        rows = slice(sc * SUPER, (sc + 1) * SUPER)
        sm = small_ref[rows, :]
        sm_t = sm.T
        for h0 in range(0, GDN_HEADS, HEAD_GROUP):
            heads = range(h0, h0 + HEAD_GROUP)
            pw, pwb, t_inv, keep = {}, {}, {}, {}
            for hd in heads:
                qs, ks, vs = head_slices(hd)
                qc = xn_ref[rows, qs]
                kc = xn_ref[rows, ks]
                vc = xn_ref[rows, vs]
                gcol = sm[:, G_LANE0 + hd:G_LANE0 + hd + 1]
                bcol = sm[:, BETA_LANE0 + hd:BETA_LANE0 + hd + 1]
                grow = sm_t[G_LANE0 + hd:G_LANE0 + hd + 1, :]
                glast = jnp.concatenate(
                    [jnp.broadcast_to(gcol[(c + 1) * CHUNK - 1:(c + 1) * CHUNK, :], (CHUNK, 1))
                     for c in range(SUPER // CHUNK)], axis=0)
                decay = jnp.exp2(LOG2E * gcol - LOG2E * grow)
                egc = jnp.exp(gcol)
                kb = kc * bcol
                kcb = kc.astype(BF16)
                pw[hd] = jnp.where(strict, _nt_dot((-kb).astype(BF16), kcb) * decay, 0.0)
                pwb[hd] = pw[hd].astype(BF16)
                t_inv[hd] = eye + pw[hd]
                attn_scr[hd, rows, :] = jnp.where(
                    incl, _nt_dot(qc.astype(BF16), kcb) * decay, 0.0).astype(BF16)
                qd_scr[rows, qs] = (qc * egc).astype(BF16)
                kd_scr[rows, qs] = (kc * jnp.exp(glast - gcol)).astype(BF16)
                keep[hd] = jnp.concatenate([vc * bcol, kb * egc], axis=1).astype(BF16)
            for _k in range(int(math.log2(CHUNK)) - 1):
                for hd in heads:
                    pwb[hd] = _dot(pwb[hd], pwb[hd]).astype(BF16)
                for hd in heads:
                    t_inv[hd] = t_inv[hd] + _dot(t_inv[hd].astype(BF16), pwb[hd])
            for hd in heads:
                qs, _, _ = head_slices(hd)
                uw = _dot(t_inv[hd].astype(BF16), keep[hd])
                u_scr[rows, qs] = uw[:, :GDN_VAL_DIM]
                w_scr[rows, qs] = uw[:, GDN_VAL_DIM:].astype(BF16)

    normw = normw_ref[...]

    def chunk_step(c, _):
        r0 = pl.multiple_of(c * CHUNK, CHUNK)
        in_sup = pl.multiple_of(lax.rem(c, SUPER // CHUNK) * CHUNK, CHUNK)
        g_tail = small_ref[pl.ds(r0 + CHUNK - SUBLANES, SUBLANES), :]
        g_end = g_tail[SUBLANES - 1:SUBLANES, :]
        states, sbs, vnbs, outs = {}, {}, {}, {}
        for hd in range(GDN_HEADS):
            qs, _, _ = head_slices(hd)
            states[hd] = state_ref[hd]
            sbs[hd] = states[hd].astype(BF16)
            v_new = u_scr[pl.ds(r0, CHUNK), qs] - _dot(w_scr[pl.ds(r0, CHUNK), qs], sbs[hd])
            vnbs[hd] = v_new.astype(BF16)
            vnew_scr[hd, pl.ds(in_sup, CHUNK), :] = vnbs[hd]
        for hd in range(GDN_HEADS):
            qs, _, _ = head_slices(hd)
            outs[hd] = (_dot(qd_scr[pl.ds(r0, CHUNK), qs], sbs[hd])
                        + _dot(attn_scr[hd, pl.ds(r0, CHUNK), :], vnew_scr[hd]))
        for hd in range(GDN_HEADS):
            qs, _, _ = head_slices(hd)
            gl = jnp.exp(g_end[:, G_LANE0 + hd:G_LANE0 + hd + 1])
            state_ref[hd] = states[hd] * gl + lax.dot_general(
                kd_scr[pl.ds(r0, CHUNK), qs], vnbs[hd], (((0,), (0,)), ((), ())),
                preferred_element_type=F32)
        for hd in range(GDN_HEADS):
            qs, _, _ = head_slices(hd)
            o = outs[hd]
            o = o * lax.rsqrt(jnp.mean(o * o, axis=-1, keepdims=True) + NORM_EPS) * normw
            zc = z_ref[pl.ds(r0, CHUNK), qs].astype(F32)
            y_ref[pl.ds(r0, CHUNK), qs] = (o * _silu(zc)).astype(y_ref.dtype)
        return 0

    lax.fori_loop(0, CHUNKS_PER_BLOCK, chunk_step, 0, unroll=2)


def _gdn_call(qkv, small, z, convw, shift, normw):
    rows = qkv.shape[0]
    n_blocks = rows // ROW_BLOCK
    hist_blocks_per_row_block = ROW_BLOCK // HIST_ROWS
    return pl.pallas_call(
        _gdn_body,
        grid=(n_blocks,),
        in_specs=[
            pl.BlockSpec((ROW_BLOCK, 3 * GDN_K_WIDTH), lambda i: (i, 0)),
            pl.BlockSpec((HIST_ROWS, 3 * GDN_K_WIDTH),
                         lambda i: (jnp.maximum(i * hist_blocks_per_row_block - 1, 0), 0)),
            pl.BlockSpec((ROW_BLOCK, LANES), lambda i: (i, 0)),
            pl.BlockSpec((ROW_BLOCK, GDN_V_WIDTH), lambda i: (i, 0)),
            _resident(convw.shape), _resident(shift.shape), _resident(normw.shape),
        ],
        out_specs=pl.BlockSpec((ROW_BLOCK, GDN_V_WIDTH), lambda i: (i, 0)),
        out_shape=jax.ShapeDtypeStruct((rows, GDN_V_WIDTH), BF16),
        scratch_shapes=[pltpu.VMEM((GDN_HEADS, GDN_KEY_DIM, GDN_VAL_DIM), F32),
                        pltpu.VMEM((ROW_BLOCK, 3 * GDN_K_WIDTH), F32),
                        pltpu.VMEM((ROW_BLOCK, GDN_V_WIDTH), F32),
                        pltpu.VMEM((ROW_BLOCK, GDN_K_WIDTH), BF16),
                        pltpu.VMEM((GDN_HEADS, ROW_BLOCK, SUPER), BF16),
                        pltpu.VMEM((ROW_BLOCK, GDN_K_WIDTH), BF16),
                        pltpu.VMEM((ROW_BLOCK, GDN_K_WIDTH), BF16),
                        pltpu.VMEM((GDN_HEADS, SUPER, GDN_VAL_DIM), BF16)],
        compiler_params=pltpu.CompilerParams(dimension_semantics=("arbitrary",),
                                             vmem_limit_bytes=VMEM_LIMIT_BYTES),
        name="gated_delta_rule",
    )(qkv, qkv, small, z, convw, shift, normw)


def _out_body(x_ref, ya_ref, yb_ref, ga_ref, gb_ref, lng_ref, lnb_ref, woa_ref, wob_ref, wo_ref,
              ln1g_ref, ln1b_ref, wgate_ref, wup_ref, wdown_ref, ln2g_ref, ln2b_ref, o_ref):
    h0 = _layer_norm(x_ref[...], lng_ref[...], lnb_ref[...])
    pa = _dot(ya_ref[...], woa_ref[...])
    pb = _dot(yb_ref[...], wob_ref[...])
    mixed = _sigmoid(ga_ref[...].astype(F32)) * pa + _sigmoid(gb_ref[...].astype(F32)) * pb
    h1 = _layer_norm(ALPHA * h0 + _dot(mixed.astype(BF16), wo_ref[...]), ln1g_ref[...], ln1b_ref[...])
    h1b = h1.astype(BF16)
    d_ff = wgate_ref.shape[1]
    half = d_ff // 2
    ffn = jnp.zeros_like(h1)
    for c in range(2):
        sl = slice(c * half, (c + 1) * half)
        gate = _dot(h1b, wgate_ref[:, sl])
        up = _dot(h1b, wup_ref[:, sl])
        ffn = ffn + _dot((_silu(gate) * up).astype(BF16), wdown_ref[sl, :])
    o_ref[...] = _layer_norm(ALPHA * h1 + ffn, ln2g_ref[...], ln2b_ref[...])


def _out_call(x, ya, yb, ga, gb, lng, lnb, woa, wob, wo, ln1g, ln1b, wgate, wup, wdown, ln2g, ln2b):
    seq = x.shape[0]
    n_blocks = seq // ROW_BLOCK
    main = lambda i: (i, 0)
    shifted = lambda i: (i + 1, 0)
    res = [lng, lnb, woa, wob, wo, ln1g, ln1b, wgate, wup, wdown, ln2g, ln2b]
    return pl.pallas_call(
        _out_body,
        grid=(n_blocks,),
        in_specs=[
            pl.BlockSpec((ROW_BLOCK, D_MODEL), main),
            pl.BlockSpec((ROW_BLOCK, ya.shape[1]), main),
            pl.BlockSpec((ROW_BLOCK, GDN_V_WIDTH), shifted),
            pl.BlockSpec((ROW_BLOCK, D_MODEL), shifted),
            pl.BlockSpec((ROW_BLOCK, D_MODEL), shifted),
        ] + [_resident(a.shape) for a in res],
        out_specs=pl.BlockSpec((ROW_BLOCK, D_MODEL), main),
        out_shape=jax.ShapeDtypeStruct((seq, D_MODEL), F32),
        compiler_params=pltpu.CompilerParams(dimension_semantics=("parallel",),
                                             vmem_limit_bytes=VMEM_LIMIT_BYTES),
        name="merge_out_ffn",
    )(x, ya, yb, ga, gb, *res)


def _slot_columns(w):
    d = w.shape[0]
    w = w.reshape(d, FOX_HEADS, FOX_HEAD_DIM)
    w = jnp.pad(w, ((0, 0), (0, 0), (0, HEAD_SLOT - FOX_HEAD_DIM)))
    return w.reshape(d, FOX_HEADS * HEAD_SLOT)


def _slot_rows(w):
    d = w.shape[1]
    w = w.reshape(FOX_HEADS, FOX_HEAD_DIM, d)
    w = jnp.pad(w, ((0, 0), (0, HEAD_SLOT - FOX_HEAD_DIM), (0, 0)))
    return w.reshape(FOX_HEADS * HEAD_SLOT, d)


def _constants():
    wide = FOX_HEADS * HEAD_SLOT
    r = np.arange(ROW_BLOCK)
    tri_full = (r[:, None] >= r[None, :])
    tri_chunk = tri_full & ((r[:, None] // CHUNK) == (r[None, :] // CHUNK))
    tri = np.concatenate([tri_full, tri_chunk], axis=0).astype(np.float32)
    e_mat = np.zeros((3 * LANES, wide), np.float32)
    for hd in range(FOX_HEADS):
        for piece in range(3):
            e_mat[piece * LANES + LOGF_LANE0 + hd, hd * HEAD_SLOT + BIAS_LANE + piece] = 1.0
    shift = np.zeros(((CONV_WIDTH - 1) * SHIFT_TILE, SHIFT_TILE + HIST_ROWS), np.float32)
    for delay in range(1, CONV_WIDTH):
        for rr in range(SHIFT_TILE):
            shift[(delay - 1) * SHIFT_TILE + rr, rr + HIST_ROWS - delay] = 1.0
    return jnp.asarray(tri, BF16), jnp.asarray(e_mat, BF16), jnp.asarray(shift, BF16)


def kernel(x, meta_tokens, ln_in_g, ln_in_b, w_in, b_f, conv_w, a_log, dt_bias, gdn_norm_w,
           w_out_a, w_out_b, w_o, ln1_g, ln1_b, w_gate, w_up, w_down, ln2_g, ln2_b):
    bsz, seq, d = x.shape
    assert bsz == 1 and d == D_MODEL and seq % Q_BLOCK == 0 and w_in.shape[0] == DEPTH
    x2 = x[0]
    wt = jnp.swapaxes(w_in[0], 0, 1)
    o = 0
    part = {}
    for name, width in (("q_a", FOX_WIDTH), ("k_a", FOX_WIDTH), ("v_a", FOX_WIDTH), ("f_a", FOX_HEADS),
                        ("qkv_b", 3 * GDN_K_WIDTH), ("a_b", GDN_HEADS), ("b_b", GDN_HEADS),
                        ("z_b", GDN_V_WIDTH), ("gate_a", D_MODEL), ("gate_b", D_MODEL)):
        part[name] = wt[o:o + width]
        o += width
    assert o == wt.shape[0]
    cols = {name: part[name].astype(BF16).T for name in ("k_a", "qkv_b", "z_b", "gate_a", "gate_b")}

    wq = (part["q_a"] * (FOX_HEAD_DIM ** -0.5 * LOG2E)).astype(BF16)
    wk = _slot_columns(cols["k_a"])
    wv = part["v_a"].astype(BF16)
    wn = jnp.zeros((LANES, d), F32)
    wn = wn.at[LOGF_LANE0:LOGF_LANE0 + FOX_HEADS].set(part["f_a"])
    wn = wn.at[G_LANE0:G_LANE0 + GDN_HEADS].set(part["a_b"])
    wn = wn.at[BETA_LANE0:BETA_LANE0 + GDN_HEADS].set(part["b_b"]).astype(BF16).T
    par = jnp.zeros((SUBLANES, LANES), F32)
    par = par.at[0, LOGF_LANE0:LOGF_LANE0 + FOX_HEADS].set(b_f[0])
    par = par.at[1, G_LANE0:G_LANE0 + GDN_HEADS].set(a_log[0])
    par = par.at[2, G_LANE0:G_LANE0 + GDN_HEADS].set(dt_bias[0])
    tri, e_mat, shift = _constants()
    prefix = jnp.zeros((PRE, d), F32).at[LEAD:].set(meta_tokens.astype(F32))
    row = lambda v: v.reshape(1, -1).astype(F32)

    qt, ka, vt, qkv, z, ga, gb, small, r_blk = _proj_call(
        prefix, x2, row(ln_in_g), row(ln_in_b), wq, wk, wv, cols["qkv_b"],
        cols["z_b"], cols["gate_a"], cols["gate_b"],
        wn, par, tri, e_mat)

    r_flat = r_blk[:, :SUBLANES].reshape(-1, SUBLANES, SUBLANES)[:, 0, :].reshape(-1)
    ya = _attn_call(r_flat, qt, ka, vt)
    yb = _gdn_call(qkv, small, z, conv_w[0].astype(F32), shift, row(gdn_norm_w[0]))
    out = _out_call(x2, ya, yb, ga, gb, row(ln_in_g), row(ln_in_b),
                    _slot_rows(w_out_a[0]).astype(BF16), w_out_b[0].astype(BF16), w_o[0].astype(BF16),
                    row(ln1_g[0]), row(ln1_b[0]), w_gate[0].astype(BF16), w_up[0].astype(BF16),
                    w_down[0].astype(BF16), row(ln2_g[0]), row(ln2_b[0]))
    return out[None]
```

```python
import functools
import math

import jax
import jax.numpy as jnp
import numpy as np
from jax import lax
from jax.experimental import pallas as pl
from jax.experimental.pallas import tpu as pltpu

D_MODEL = 1024
CHUNK = 64
N_META = 16
FOX_HEADS = 8
FOX_HEAD_DIM = 64
FOX_WIDTH = FOX_HEADS * FOX_HEAD_DIM
GDN_HEADS = 8
GDN_KEY_DIM = 128
GDN_VAL_DIM = 128
GDN_K_WIDTH = GDN_HEADS * GDN_KEY_DIM
GDN_V_WIDTH = GDN_HEADS * GDN_VAL_DIM
CONV_WIDTH = 4
DEPTH = 1
ALPHA = (2.0 * DEPTH) ** 0.25
LN_EPS = 1e-5
NORM_EPS = 1e-6

LANES = 128
SUBLANES = 8
VMEM_LIMIT_BYTES = 56 * 1024 * 1024

ROW_BLOCK = 512
PRE = ROW_BLOCK
LEAD = PRE - N_META
CHUNKS_PER_BLOCK = ROW_BLOCK // CHUNK
Q_BLOCK = 2 * ROW_BLOCK
SUPER = 128
HEAD_GROUP = 4
HEAD_SLOT = LANES
BIAS_LANE = FOX_HEAD_DIM
V_ROWS = FOX_HEAD_DIM + 16
MASK_BIAS = -1e30
NEG_INIT = -1e30
LOG2E = math.log2(math.e)
F32 = jnp.float32
BF16 = jnp.bfloat16

LOGF_LANE0 = 0
G_LANE0 = 8
BETA_LANE0 = 16


def _nt_dot(a, b):
    return lax.dot_general(a, b, (((1,), (1,)), ((), ())), preferred_element_type=F32)


def _dot(a, b):
    return jnp.dot(a, b, preferred_element_type=F32)


def _layer_norm(x, g, b):
    mu = jnp.mean(x, axis=-1, keepdims=True)
    xc = x - mu
    var = jnp.mean(xc * xc, axis=-1, keepdims=True)
    return xc * lax.rsqrt(var + LN_EPS) * g + b


def _softplus(x):
    return jnp.maximum(x, 0.0) + jnp.log(1.0 + jnp.exp(-jnp.abs(x)))


def _sigmoid(x):
    return 1.0 / (1.0 + jnp.exp(-x))


def _silu(x):
    return x * _sigmoid(x)


def _split3(x):
    hi = x.astype(BF16)
    r1 = x - hi.astype(F32)
    mid = r1.astype(BF16)
    lo = (r1 - mid.astype(F32)).astype(BF16)
    return hi, mid, lo


def _resident(shape):
    nd = len(shape)
    return pl.BlockSpec(shape, lambda *_: (0,) * nd, pipeline_mode=pl.Buffered(1))


def _proj_body(prefix_ref, x_ref, lng_ref, lnb_ref, wq_ref, wk_ref, wv_ref, wg_ref, wz_ref, wga_ref,
               wgb_ref, wn_ref, par_ref, tri_ref, e_ref,
               qt_ref, ka_ref, vt_ref, qkv_ref, z_ref, ga_ref, gb_ref, small_ref, r_ref,
               carry_ref):
    i = pl.program_id(0)

    @pl.when(i == 0)
    def _():
        carry_ref[...] = jnp.zeros_like(carry_ref)

    src = jnp.where(i == 0, prefix_ref[...], x_ref[...])
    h = _layer_norm(src, lng_ref[...], lnb_ref[...])
    row = lax.broadcasted_iota(jnp.int32, (ROW_BLOCK, 1), 0)
    valid = jnp.logical_or(i > 0, row >= LEAD)
    hb = jnp.where(valid, h, 0.0).astype(BF16)

    nar = _nt_dot(hb, wn_ref[...])
    lane = lax.broadcasted_iota(jnp.int32, (1, LANES), 1)
    b_f = par_ref[0:1, :]
    a_log = par_ref[1:2, :]
    dt_bias = par_ref[2:3, :]
    xf = nar + b_f
    logf = jnp.minimum(xf, 0.0) - jnp.log(1.0 + jnp.exp(-jnp.abs(xf)))
    g = -jnp.exp(a_log) * _softplus(nar + dt_bias)
    beta = _sigmoid(nar)
    is_logf = lane < G_LANE0
    is_g = jnp.logical_and(lane >= G_LANE0, lane < BETA_LANE0)
    vals = jnp.where(is_logf, logf, jnp.where(is_g, g, 0.0))

    hi, mid, lo = _split3(vals)
    tri = tri_ref[...]
    cs = _dot(tri, hi) + _dot(tri, mid) + _dot(tri, lo)
    d_blk = cs[:ROW_BLOCK]
    gc = cs[ROW_BLOCK:]
    carry = carry_ref[...]
    r_ref[...] = jnp.broadcast_to(carry, r_ref.shape)
    carry_ref[...] = carry + d_blk[ROW_BLOCK - 1:ROW_BLOCK, :]
    small_ref[...] = jnp.where(is_g, gc, jnp.where(lane >= BETA_LANE0, beta, d_blk))

    nd = jnp.where(valid, -LOG2E * d_blk, MASK_BIAS)
    nd = jnp.where(is_logf, nd, 0.0)
    phi, pmid, plo = _split3(nd)
    pieces = jnp.concatenate([phi, pmid, plo], axis=1)
    qt_ref[0] = _nt_dot(wq_ref[...], hb).astype(BF16)
    ka_ref[...] = (_nt_dot(hb, wk_ref[...]) + _dot(pieces, e_ref[...])).astype(BF16)
    vt_ref[0] = _nt_dot(wv_ref[...], hb).astype(BF16)

    for c in range(3):
        sl = slice(c * GDN_K_WIDTH, (c + 1) * GDN_K_WIDTH)
        qkv_ref[:, sl] = _nt_dot(hb, wg_ref[sl, :]).astype(BF16)
    z_ref[...] = _nt_dot(hb, wz_ref[...]).astype(BF16)
    ga_ref[...] = _nt_dot(hb, wga_ref[...]).astype(BF16)
    gb_ref[...] = _nt_dot(hb, wgb_ref[...]).astype(BF16)


def _proj_call(prefix, x, lng, lnb, wq, wk, wv, wg, wz, wga, wgb, wn, par, tri, e_mat):
    seq = x.shape[0]
    n_blocks = seq // ROW_BLOCK + 1
    rows = n_blocks * ROW_BLOCK
    wide = FOX_HEADS * HEAD_SLOT

    def row_spec(width):
        return pl.BlockSpec((ROW_BLOCK, width), lambda i: (i, 0))

    in_specs = [
        _resident(prefix.shape),
        pl.BlockSpec((ROW_BLOCK, D_MODEL), lambda i: (jnp.maximum(i - 1, 0), 0)),
        _resident(lng.shape), _resident(lnb.shape),
        _resident(wq.shape), _resident(wk.shape), _resident(wv.shape), _resident(wg.shape),
        _resident(wz.shape), _resident(wga.shape), _resident(wgb.shape), _resident(wn.shape),
        _resident(par.shape), _resident(tri.shape), _resident(e_mat.shape),
    ]
    out_shape = [
        jax.ShapeDtypeStruct((n_blocks - 1, FOX_WIDTH, ROW_BLOCK), BF16),
        jax.ShapeDtypeStruct((rows, wide), BF16),
        jax.ShapeDtypeStruct((n_blocks, FOX_WIDTH, ROW_BLOCK), BF16),
        jax.ShapeDtypeStruct((rows, 3 * GDN_K_WIDTH), BF16),
        jax.ShapeDtypeStruct((rows, GDN_V_WIDTH), BF16),
        jax.ShapeDtypeStruct((rows, D_MODEL), BF16),
        jax.ShapeDtypeStruct((rows, D_MODEL), BF16),
        jax.ShapeDtypeStruct((rows, LANES), F32),
        jax.ShapeDtypeStruct((n_blocks * SUBLANES, LANES), F32),
    ]
    q_spec = pl.BlockSpec((1, FOX_WIDTH, ROW_BLOCK), lambda i: (jnp.maximum(i - 1, 0), 0, 0))
    v_spec = pl.BlockSpec((1, FOX_WIDTH, ROW_BLOCK), lambda i: (i, 0, 0))
    out_specs = [q_spec, row_spec(wide), v_spec, row_spec(3 * GDN_K_WIDTH),
                 row_spec(GDN_V_WIDTH), row_spec(D_MODEL), row_spec(D_MODEL), row_spec(LANES),
                 pl.BlockSpec((SUBLANES, LANES), lambda i: (i, 0))]
    return pl.pallas_call(
        _proj_body,
        grid=(n_blocks,),
        in_specs=in_specs,
        out_specs=out_specs,
        out_shape=out_shape,
        scratch_shapes=[pltpu.VMEM((1, LANES), F32)],
        compiler_params=pltpu.CompilerParams(dimension_semantics=("arbitrary",),
                                             vmem_limit_bytes=VMEM_LIMIT_BYTES),
        name="ln_in_proj",
    )(prefix, x, lng, lnb, wq, wk, wv, wg, wz, wga, wgb, wn, par, tri, e_mat)


def _attn_body(r_ref, qt_ref, k_ref, vt_ref, o_ref, s_scr, mx_scr, m_scr, acc_scr):
    h = pl.program_id(0)
    i = pl.program_id(1)
    q_t = jnp.concatenate([qt_ref[b] for b in range(Q_BLOCK // ROW_BLOCK)], axis=1)
    pad_row = lax.broadcasted_iota(jnp.int32, (HEAD_SLOT - FOX_HEAD_DIM, Q_BLOCK), 0)
    q_t = jnp.concatenate([q_t, jnp.where(pad_row < 3, 1.0, 0.0).astype(BF16)], axis=0)
    one_row = lax.broadcasted_iota(jnp.int32, (V_ROWS - FOX_HEAD_DIM, ROW_BLOCK), 0)
    v_ones = jnp.where(one_row == 0, 1.0, 0.0).astype(BF16)

    def produce(slot, j, mask=None):
        start = pl.multiple_of(j * ROW_BLOCK, ROW_BLOCK)
        s = _dot(k_ref[pl.ds(start, ROW_BLOCK), :], q_t)
        if mask is not None:
            s = jnp.where(mask, s, MASK_BIAS)
        s_scr[slot] = s
        mx_scr[slot] = jnp.max(s.reshape(ROW_BLOCK // SUBLANES, SUBLANES, Q_BLOCK), axis=0)

    def consume(slot, j):
        rj = r_ref[j * SUBLANES + h] * LOG2E
        m = m_scr[...]
        m_new = jnp.maximum(m, jnp.max(mx_scr[slot], axis=0, keepdims=True) - rj)
        p = jnp.exp2(s_scr[slot] - (m_new + rj))
        alpha = jnp.exp2(m - m_new)
        v_t = jnp.concatenate([vt_ref[j], v_ones], axis=0)
        acc_scr[...] = alpha * acc_scr[...] + _dot(v_t, p.astype(BF16))
        m_scr[...] = m_new

    m_scr[...] = jnp.full(m_scr.shape, NEG_INIT, F32)
    acc_scr[...] = jnp.zeros(acc_scr.shape, F32)
    produce(0, 0)

    def pair(j):
        produce(1, j + 1)
        consume(0, j)
        produce(0, j + 2)
        consume(1, j + 1)

    def quad(t, _):
        pair(4 * t)
        pair(4 * t + 2)
        return 0

    lax.fori_loop(0, i // 2, quad, 0)

    @pl.when(lax.rem(i, 2) == 1)
    def _():
        pair(2 * i - 2)

    first_diag = 2 * i + 1
    key = lax.broadcasted_iota(jnp.int32, (ROW_BLOCK, Q_BLOCK), 0)
    qry = lax.broadcasted_iota(jnp.int32, (ROW_BLOCK, Q_BLOCK), 1)
    produce(1, first_diag, qry >= key)
    consume(0, first_diag - 1)
    produce(0, first_diag + 1, qry >= key + ROW_BLOCK)
    consume(1, first_diag)
    consume(0, first_diag + 1)

    acc = acc_scr[...]
    denom = acc[FOX_HEAD_DIM:FOX_HEAD_DIM + 1, :]
    out_t = jnp.concatenate([acc[:FOX_HEAD_DIM] / denom,
                             jnp.zeros((HEAD_SLOT - FOX_HEAD_DIM, Q_BLOCK), F32)], axis=0)
    o_ref[...] = out_t.T.astype(o_ref.dtype)


def _attn_call(r_flat, qt, ka, vt):
    n_kv = vt.shape[0]
    n_q = qt.shape[0] * ROW_BLOCK // Q_BLOCK
    return pl.pallas_call(
        _attn_body,
        grid=(FOX_HEADS, n_q),
        in_specs=[
            pl.BlockSpec(memory_space=pltpu.SMEM),
            pl.BlockSpec((Q_BLOCK // ROW_BLOCK, FOX_HEAD_DIM, ROW_BLOCK), lambda h, i: (i, h, 0)),
            pl.BlockSpec((n_kv * ROW_BLOCK, HEAD_SLOT), lambda h, i: (0, h)),
            pl.BlockSpec((n_kv, FOX_HEAD_DIM, ROW_BLOCK), lambda h, i: (0, h, 0)),
        ],
        out_specs=pl.BlockSpec((Q_BLOCK, HEAD_SLOT), lambda h, i: (i, h)),
        out_shape=jax.ShapeDtypeStruct((n_q * Q_BLOCK, FOX_HEADS * HEAD_SLOT), BF16),
        scratch_shapes=[pltpu.VMEM((2, ROW_BLOCK, Q_BLOCK), F32),
                        pltpu.VMEM((2, SUBLANES, Q_BLOCK), F32),
                        pltpu.VMEM((1, Q_BLOCK), F32),
                        pltpu.VMEM((V_ROWS, Q_BLOCK), F32)],
        compiler_params=pltpu.CompilerParams(dimension_semantics=("arbitrary", "arbitrary"),
                                             vmem_limit_bytes=VMEM_LIMIT_BYTES),
        name="fox_attention",
    )(r_flat, qt, ka, vt)


def _gdn_body(qkv_ref, hist_ref, small_ref, z_ref, convw_ref, normw_ref, y_ref,
              state_ref, xn_ref, u_scr, w_scr, attn_scr, qd_scr, kd_scr, vnew_scr):
    i = pl.program_id(0)

    @pl.when(i == 0)
    def _():
        state_ref[...] = jnp.zeros_like(state_ref)
        vnew_scr[...] = jnp.zeros_like(vnew_scr)

    hist = jnp.where(i > 0, hist_ref[...].astype(F32), 0.0)
    cur = qkv_ref[...].astype(F32)
    ext = jnp.concatenate([hist, cur], axis=0)
    conv = convw_ref[CONV_WIDTH - 1:CONV_WIDTH, :] * cur
    for t in range(CONV_WIDTH - 1):
        delay = CONV_WIDTH - 1 - t
        shifted = pltpu.roll(ext, delay, axis=0)[SUBLANES:, :]
        conv = conv + convw_ref[t:t + 1, :] * shifted
    act = _silu(conv)
    for hd in range(2 * GDN_HEADS):
        sl = slice(hd * GDN_KEY_DIM, (hd + 1) * GDN_KEY_DIM)
        blk = act[:, sl]
        scale = lax.rsqrt(jnp.sum(blk * blk, axis=-1, keepdims=True) + NORM_EPS)
        if hd < GDN_HEADS:
            scale = scale * (GDN_KEY_DIM ** -0.5)
        xn_ref[:, sl] = blk * scale
    xn_ref[:, 2 * GDN_K_WIDTH:] = act[:, 2 * GDN_K_WIDTH:]

    ri = lax.broadcasted_iota(jnp.int32, (SUPER, SUPER), 0)
    ci = lax.broadcasted_iota(jnp.int32, (SUPER, SUPER), 1)
    same = (ri // CHUNK) == (ci // CHUNK)
    incl = jnp.logical_and(same, ri >= ci)
    strict = jnp.logical_and(same, ri > ci)
    eye = (ri == ci).astype(F32)

    def head_slices(hd):
        qs = slice(hd * GDN_KEY_DIM, (hd + 1) * GDN_KEY_DIM)
        ks = slice(GDN_K_WIDTH + hd * GDN_KEY_DIM, GDN_K_WIDTH + (hd + 1) * GDN_KEY_DIM)
        vs = slice(2 * GDN_K_WIDTH + hd * GDN_VAL_DIM, 2 * GDN_K_WIDTH + (hd + 1) * GDN_VAL_DIM)
        return qs, ks, vs

    for sc in range(ROW_BLOCK // SUPER):
        rows = slice(sc * SUPER, (sc + 1) * SUPER)
        sm = small_ref[rows, :]
        sm_t = sm.T
        for h0 in range(0, GDN_HEADS, HEAD_GROUP):
            heads = range(h0, h0 + HEAD_GROUP)
            pw, pwb, t_inv, keep = {}, {}, {}, {}
            for hd in heads:
                qs, ks, vs = head_slices(hd)
                qc = xn_ref[rows, qs]
                kc = xn_ref[rows, ks]
                vc = xn_ref[rows, vs]
                gcol = sm[:, G_LANE0 + hd:G_LANE0 + hd + 1]
                bcol = sm[:, BETA_LANE0 + hd:BETA_LANE0 + hd + 1]
                grow = sm_t[G_LANE0 + hd:G_LANE0 + hd + 1, :]
                glast = jnp.concatenate(
                    [jnp.broadcast_to(gcol[(c + 1) * CHUNK - 1:(c + 1) * CHUNK, :], (CHUNK, 1))
                     for c in range(SUPER // CHUNK)], axis=0)
                decay = jnp.exp2(LOG2E * gcol - LOG2E * grow)
                egc = jnp.exp(gcol)
                kb = kc * bcol
                kcb = kc.astype(BF16)
                pw[hd] = jnp.where(strict, _nt_dot((-kb).astype(BF16), kcb) * decay, 0.0)
                pwb[hd] = pw[hd].astype(BF16)
                t_inv[hd] = eye + pw[hd]
                attn_scr[hd, rows, :] = jnp.where(
                    incl, _nt_dot(qc.astype(BF16), kcb) * decay, 0.0).astype(BF16)
                qd_scr[rows, qs] = (qc * egc).astype(BF16)
                kd_scr[rows, qs] = (kc * jnp.exp(glast - gcol)).astype(BF16)
                keep[hd] = jnp.concatenate([vc * bcol, kb * egc], axis=1).astype(BF16)
            for _k in range(int(math.log2(CHUNK)) - 1):
                for hd in heads:
                    pwb[hd] = _dot(pwb[hd], pwb[hd]).astype(BF16)
                for hd in heads:
                    t_inv[hd] = t_inv[hd] + _dot(t_inv[hd].astype(BF16), pwb[hd])
            for hd in heads:
                qs, _, _ = head_slices(hd)
                uw = _dot(t_inv[hd].astype(BF16), keep[hd])
                u_scr[rows, qs] = uw[:, :GDN_VAL_DIM]
                w_scr[rows, qs] = uw[:, GDN_VAL_DIM:].astype(BF16)

    normw = normw_ref[...]

    def chunk_step(c, _):
        r0 = pl.multiple_of(c * CHUNK, CHUNK)
        in_sup = pl.multiple_of(lax.rem(c, SUPER // CHUNK) * CHUNK, CHUNK)
        g_tail = small_ref[pl.ds(r0 + CHUNK - SUBLANES, SUBLANES), :]
        g_end = g_tail[SUBLANES - 1:SUBLANES, :]
        states, sbs, vnbs, outs = {}, {}, {}, {}
        for hd in range(GDN_HEADS):
            qs, _, _ = head_slices(hd)
            states[hd] = state_ref[hd]
            sbs[hd] = states[hd].astype(BF16)
            v_new = u_scr[pl.ds(r0, CHUNK), qs] - _dot(w_scr[pl.ds(r0, CHUNK), qs], sbs[hd])
            vnbs[hd] = v_new.astype(BF16)
            vnew_scr[hd, pl.ds(in_sup, CHUNK), :] = vnbs[hd]
        for hd in range(GDN_HEADS):
            qs, _, _ = head_slices(hd)
            outs[hd] = (_dot(qd_scr[pl.ds(r0, CHUNK), qs], sbs[hd])
                        + _dot(attn_scr[hd, pl.ds(r0, CHUNK), :], vnew_scr[hd]))
        for hd in range(GDN_HEADS):
            qs, _, _ = head_slices(hd)
            gl = jnp.exp(g_end[:, G_LANE0 + hd:G_LANE0 + hd + 1])
            state_ref[hd] = states[hd] * gl + lax.dot_general(
                kd_scr[pl.ds(r0, CHUNK), qs], vnbs[hd], (((0,), (0,)), ((), ())),
                preferred_element_type=F32)
        for hd in range(GDN_HEADS):
            qs, _, _ = head_slices(hd)
            o = outs[hd]
            o = o * lax.rsqrt(jnp.mean(o * o, axis=-1, keepdims=True) + NORM_EPS) * normw
            zc = z_ref[pl.ds(r0, CHUNK), qs].astype(F32)
            y_ref[pl.ds(r0, CHUNK), qs] = (o * _silu(zc)).astype(y_ref.dtype)
        return 0

    lax.fori_loop(0, CHUNKS_PER_BLOCK, chunk_step, 0, unroll=2)


def _gdn_call(qkv, small, z, convw, normw):
    rows = qkv.shape[0]
    n_blocks = rows // ROW_BLOCK
    hist_blocks_per_row_block = ROW_BLOCK // SUBLANES
    return pl.pallas_call(
        _gdn_body,
        grid=(n_blocks,),
        in_specs=[
            pl.BlockSpec((ROW_BLOCK, 3 * GDN_K_WIDTH), lambda i: (i, 0)),
            pl.BlockSpec((SUBLANES, 3 * GDN_K_WIDTH),
                         lambda i: (jnp.maximum(i * hist_blocks_per_row_block - 1, 0), 0)),
            pl.BlockSpec((ROW_BLOCK, LANES), lambda i: (i, 0)),
            pl.BlockSpec((ROW_BLOCK, GDN_V_WIDTH), lambda i: (i, 0)),
            _resident(convw.shape), _resident(normw.shape),
        ],
        out_specs=pl.BlockSpec((ROW_BLOCK, GDN_V_WIDTH), lambda i: (i, 0)),
        out_shape=jax.ShapeDtypeStruct((rows, GDN_V_WIDTH), BF16),
        scratch_shapes=[pltpu.VMEM((GDN_HEADS, GDN_KEY_DIM, GDN_VAL_DIM), F32),
                        pltpu.VMEM((ROW_BLOCK, 3 * GDN_K_WIDTH), F32),
                        pltpu.VMEM((ROW_BLOCK, GDN_V_WIDTH), F32),
                        pltpu.VMEM((ROW_BLOCK, GDN_K_WIDTH), BF16),
                        pltpu.VMEM((GDN_HEADS, ROW_BLOCK, SUPER), BF16),
                        pltpu.VMEM((ROW_BLOCK, GDN_K_WIDTH), BF16),
                        pltpu.VMEM((ROW_BLOCK, GDN_K_WIDTH), BF16),
                        pltpu.VMEM((GDN_HEADS, SUPER, GDN_VAL_DIM), BF16)],
        compiler_params=pltpu.CompilerParams(dimension_semantics=("arbitrary",),
                                             vmem_limit_bytes=VMEM_LIMIT_BYTES),
        name="gated_delta_rule",
    )(qkv, qkv, small, z, convw, normw)


def _out_body(x_ref, ya_ref, yb_ref, ga_ref, gb_ref, lng_ref, lnb_ref, woa_ref, wob_ref, wo_ref,
              ln1g_ref, ln1b_ref, wgate_ref, wup_ref, wdown_ref, ln2g_ref, ln2b_ref, o_ref):
    h0 = _layer_norm(x_ref[...], lng_ref[...], lnb_ref[...])
    pa = _dot(ya_ref[...], woa_ref[...])
    pb = _dot(yb_ref[...], wob_ref[...])
    mixed = _sigmoid(ga_ref[...].astype(F32)) * pa + _sigmoid(gb_ref[...].astype(F32)) * pb
    h1 = _layer_norm(ALPHA * h0 + _dot(mixed.astype(BF16), wo_ref[...]), ln1g_ref[...], ln1b_ref[...])
    h1b = h1.astype(BF16)
    d_ff = wgate_ref.shape[1]
    half = d_ff // 2
    ffn = jnp.zeros_like(h1)
    for c in range(2):
        sl = slice(c * half, (c + 1) * half)
        gate = _dot(h1b, wgate_ref[:, sl])
        up = _dot(h1b, wup_ref[:, sl])
        ffn = ffn + _dot((_silu(gate) * up).astype(BF16), wdown_ref[sl, :])
    o_ref[...] = _layer_norm(ALPHA * h1 + ffn, ln2g_ref[...], ln2b_ref[...])


def _out_call(x, ya, yb, ga, gb, lng, lnb, woa, wob, wo, ln1g, ln1b, wgate, wup, wdown, ln2g, ln2b):
    seq = x.shape[0]
    n_blocks = seq // ROW_BLOCK
    main = lambda i: (i, 0)
    shifted = lambda i: (i + 1, 0)
    res = [lng, lnb, woa, wob, wo, ln1g, ln1b, wgate, wup, wdown, ln2g, ln2b]
    return pl.pallas_call(
        _out_body,
        grid=(n_blocks,),
        in_specs=[
            pl.BlockSpec((ROW_BLOCK, D_MODEL), main),
            pl.BlockSpec((ROW_BLOCK, ya.shape[1]), main),
            pl.BlockSpec((ROW_BLOCK, GDN_V_WIDTH), shifted),
            pl.BlockSpec((ROW_BLOCK, D_MODEL), shifted),
            pl.BlockSpec((ROW_BLOCK, D_MODEL), shifted),
        ] + [_resident(a.shape) for a in res],
        out_specs=pl.BlockSpec((ROW_BLOCK, D_MODEL), main),
        out_shape=jax.ShapeDtypeStruct((seq, D_MODEL), F32),
        compiler_params=pltpu.CompilerParams(dimension_semantics=("parallel",),
                                             vmem_limit_bytes=VMEM_LIMIT_BYTES),
        name="merge_out_ffn",
    )(x, ya, yb, ga, gb, *res)


def _slot_rows(w):
    d = w.shape[1]
    w = w.reshape(FOX_HEADS, FOX_HEAD_DIM, d)
    w = jnp.pad(w, ((0, 0), (0, HEAD_SLOT - FOX_HEAD_DIM), (0, 0)))
    return w.reshape(FOX_HEADS * HEAD_SLOT, d)


def _constants():
    wide = FOX_HEADS * HEAD_SLOT
    r = np.arange(ROW_BLOCK)
    tri_full = (r[:, None] >= r[None, :])
    tri_chunk = tri_full & ((r[:, None] // CHUNK) == (r[None, :] // CHUNK))
    tri = np.concatenate([tri_full, tri_chunk], axis=0).astype(np.float32)
    e_mat = np.zeros((3 * LANES, wide), np.float32)
    for hd in range(FOX_HEADS):
        for piece in range(3):
            e_mat[piece * LANES + LOGF_LANE0 + hd, hd * HEAD_SLOT + BIAS_LANE + piece] = 1.0
    return jnp.asarray(tri, BF16), jnp.asarray(e_mat, BF16)


def kernel(x, meta_tokens, ln_in_g, ln_in_b, w_in, b_f, conv_w, a_log, dt_bias, gdn_norm_w,
           w_out_a, w_out_b, w_o, ln1_g, ln1_b, w_gate, w_up, w_down, ln2_g, ln2_b):
    bsz, seq, d = x.shape
    assert bsz == 1 and d == D_MODEL and seq % Q_BLOCK == 0 and w_in.shape[0] == DEPTH
    x2 = x[0]
    wt = jnp.swapaxes(w_in[0], 0, 1)
    o = 0
    part = {}
    for name, width in (("q_a", FOX_WIDTH), ("k_a", FOX_WIDTH), ("v_a", FOX_WIDTH), ("f_a", FOX_HEADS),
                        ("qkv_b", 3 * GDN_K_WIDTH), ("a_b", GDN_HEADS), ("b_b", GDN_HEADS),
                        ("z_b", GDN_V_WIDTH), ("gate_a", D_MODEL), ("gate_b", D_MODEL)):
        part[name] = wt[o:o + width]
        o += width
    assert o == wt.shape[0]
    cols = {name: part[name].astype(BF16) for name in ("k_a", "qkv_b", "z_b", "gate_a", "gate_b")}

    wq = (part["q_a"] * (FOX_HEAD_DIM ** -0.5 * LOG2E)).astype(BF16)
    wk = _slot_rows(cols["k_a"])
    wv = part["v_a"].astype(BF16)
    wn = jnp.zeros((LANES, d), F32)
    wn = wn.at[LOGF_LANE0:LOGF_LANE0 + FOX_HEADS].set(part["f_a"])
    wn = wn.at[G_LANE0:G_LANE0 + GDN_HEADS].set(part["a_b"])
    wn = wn.at[BETA_LANE0:BETA_LANE0 + GDN_HEADS].set(part["b_b"]).astype(BF16)
    par = jnp.zeros((SUBLANES, LANES), F32)
    par = par.at[0, LOGF_LANE0:LOGF_LANE0 + FOX_HEADS].set(b_f[0])
    par = par.at[1, G_LANE0:G_LANE0 + GDN_HEADS].set(a_log[0])
    par = par.at[2, G_LANE0:G_LANE0 + GDN_HEADS].set(dt_bias[0])
    tri, e_mat = _constants()
    prefix = jnp.zeros((PRE, d), F32).at[LEAD:].set(meta_tokens.astype(F32))
    row = lambda v: v.reshape(1, -1).astype(F32)

    qt, ka, vt, qkv, z, ga, gb, small, r_blk = _proj_call(
        prefix, x2, row(ln_in_g), row(ln_in_b), wq, wk, wv, cols["qkv_b"],
        cols["z_b"], cols["gate_a"], cols["gate_b"],
        wn, par, tri, e_mat)

    r_flat = r_blk[:, :SUBLANES].reshape(-1, SUBLANES, SUBLANES)[:, 0, :].reshape(-1)
    ya = _attn_call(r_flat, qt, ka, vt)
    yb = _gdn_call(qkv, small, z, conv_w[0].astype(F32), row(gdn_norm_w[0]))
    out = _out_call(x2, ya, yb, ga, gb, row(ln_in_g), row(ln_in_b),
                    _slot_rows(w_out_a[0]).astype(BF16), w_out_b[0].astype(BF16), w_o[0].astype(BF16),
                    row(ln1_g[0]), row(ln1_b[0]), w_gate[0].astype(BF16), w_up[0].astype(BF16),
                    w_down[0].astype(BF16), row(ln2_g[0]), row(ln2_b[0]))
    return out[None]
```

```python
import functools
import math

import jax
import jax.numpy as jnp
import numpy as np
from jax import lax
from jax.experimental import pallas as pl
from jax.experimental.pallas import tpu as pltpu

D_MODEL = 1024
CHUNK = 64
N_META = 16
FOX_HEADS = 8
FOX_HEAD_DIM = 64
FOX_WIDTH = FOX_HEADS * FOX_HEAD_DIM
GDN_HEADS = 8
GDN_KEY_DIM = 128
GDN_VAL_DIM = 128
GDN_K_WIDTH = GDN_HEADS * GDN_KEY_DIM
GDN_V_WIDTH = GDN_HEADS * GDN_VAL_DIM
CONV_WIDTH = 4
DEPTH = 1
ALPHA = (2.0 * DEPTH) ** 0.25
LN_EPS = 1e-5
NORM_EPS = 1e-6

LANES = 128
SUBLANES = 8
VMEM_LIMIT_BYTES = 56 * 1024 * 1024

ROW_BLOCK = 512
PRE = ROW_BLOCK
LEAD = PRE - N_META
CHUNKS_PER_BLOCK = ROW_BLOCK // CHUNK
Q_BLOCK = 2 * ROW_BLOCK
SUPER = 128
HEAD_GROUP = 4
HEAD_SLOT = LANES
BIAS_LANE = FOX_HEAD_DIM
V_ROWS = FOX_HEAD_DIM + 16
MASK_BIAS = -1e30
NEG_INIT = -1e30
LOG2E = math.log2(math.e)
F32 = jnp.float32
BF16 = jnp.bfloat16

LOGF_LANE0 = 0
G_LANE0 = 8
BETA_LANE0 = 16


def _nt_dot(a, b):
    return lax.dot_general(a, b, (((1,), (1,)), ((), ())), preferred_element_type=F32)


def _dot(a, b):
    return jnp.dot(a, b, preferred_element_type=F32)


def _layer_norm(x, g, b):
    mu = jnp.mean(x, axis=-1, keepdims=True)
    xc = x - mu
    var = jnp.mean(xc * xc, axis=-1, keepdims=True)
    return xc * lax.rsqrt(var + LN_EPS) * g + b


def _softplus(x):
    return jnp.maximum(x, 0.0) + jnp.log(1.0 + jnp.exp(-jnp.abs(x)))


def _sigmoid(x):
    return 1.0 / (1.0 + jnp.exp(-x))


def _silu(x):
    return x * _sigmoid(x)


def _split3(x):
    hi = x.astype(BF16)
    r1 = x - hi.astype(F32)
    mid = r1.astype(BF16)
    lo = (r1 - mid.astype(F32)).astype(BF16)
    return hi, mid, lo


def _resident(shape):
    nd = len(shape)
    return pl.BlockSpec(shape, lambda *_: (0,) * nd, pipeline_mode=pl.Buffered(1))


def _proj_body(prefix_ref, x_ref, lng_ref, lnb_ref, wq_ref, wk_ref, wv_ref, wg_ref, wz_ref, wga_ref,
               wgb_ref, wn_ref, par_ref, tri_ref, e_ref,
               qt_ref, ka_ref, vt_ref, qkv_ref, z_ref, ga_ref, gb_ref, small_ref, r_ref,
               carry_ref):
    i = pl.program_id(0)

    @pl.when(i == 0)
    def _():
        carry_ref[...] = jnp.zeros_like(carry_ref)

    src = jnp.where(i == 0, prefix_ref[...], x_ref[...])
    h = _layer_norm(src, lng_ref[...], lnb_ref[...])
    row = lax.broadcasted_iota(jnp.int32, (ROW_BLOCK, 1), 0)
    valid = jnp.logical_or(i > 0, row >= LEAD)
    hb = jnp.where(valid, h, 0.0).astype(BF16)

    nar = _nt_dot(hb, wn_ref[...])
    lane = lax.broadcasted_iota(jnp.int32, (1, LANES), 1)
    b_f = par_ref[0:1, :]
    a_log = par_ref[1:2, :]
    dt_bias = par_ref[2:3, :]
    xf = nar + b_f
    logf = jnp.minimum(xf, 0.0) - jnp.log(1.0 + jnp.exp(-jnp.abs(xf)))
    g = -jnp.exp(a_log) * _softplus(nar + dt_bias)
    beta = _sigmoid(nar)
    is_logf = lane < G_LANE0
    is_g = jnp.logical_and(lane >= G_LANE0, lane < BETA_LANE0)
    vals = jnp.where(is_logf, logf, jnp.where(is_g, g, 0.0))

    hi, mid, lo = _split3(vals)
    tri = tri_ref[...]
    cs = _dot(tri, hi) + _dot(tri, mid) + _dot(tri, lo)
    d_blk = cs[:ROW_BLOCK]
    gc = cs[ROW_BLOCK:]
    carry = carry_ref[...]
    r_ref[...] = jnp.broadcast_to(carry, r_ref.shape)
    carry_ref[...] = carry + d_blk[ROW_BLOCK - 1:ROW_BLOCK, :]
    small_ref[...] = jnp.where(is_g, gc, jnp.where(lane >= BETA_LANE0, beta, d_blk))

    nd = jnp.where(valid, -LOG2E * d_blk, MASK_BIAS)
    nd = jnp.where(is_logf, nd, 0.0)
    phi, pmid, plo = _split3(nd)
    pieces = jnp.concatenate([phi, pmid, plo], axis=1)
    qt_ref[0] = _nt_dot(wq_ref[...], hb).astype(BF16)
    ka_ref[...] = (_nt_dot(hb, wk_ref[...]) + _dot(pieces, e_ref[...])).astype(BF16)
    vt_ref[0] = _nt_dot(wv_ref[...], hb).astype(BF16)

    for c in range(3):
        sl = slice(c * GDN_K_WIDTH, (c + 1) * GDN_K_WIDTH)
        qkv_ref[:, sl] = _nt_dot(hb, wg_ref[sl, :]).astype(BF16)
    z_ref[...] = _nt_dot(hb, wz_ref[...]).astype(BF16)
    ga_ref[...] = _nt_dot(hb, wga_ref[...]).astype(BF16)
    gb_ref[...] = _nt_dot(hb, wgb_ref[...]).astype(BF16)


def _proj_call(prefix, x, lng, lnb, wq, wk, wv, wg, wz, wga, wgb, wn, par, tri, e_mat):
    seq = x.shape[0]
    n_blocks = seq // ROW_BLOCK + 1
    rows = n_blocks * ROW_BLOCK
    wide = FOX_HEADS * HEAD_SLOT

    def row_spec(width):
        return pl.BlockSpec((ROW_BLOCK, width), lambda i: (i, 0))

    in_specs = [
        _resident(prefix.shape),
        pl.BlockSpec((ROW_BLOCK, D_MODEL), lambda i: (jnp.maximum(i - 1, 0), 0)),
        _resident(lng.shape), _resident(lnb.shape),
        _resident(wq.shape), _resident(wk.shape), _resident(wv.shape), _resident(wg.shape),
        _resident(wz.shape), _resident(wga.shape), _resident(wgb.shape), _resident(wn.shape),
        _resident(par.shape), _resident(tri.shape), _resident(e_mat.shape),
    ]
    out_shape = [
        jax.ShapeDtypeStruct((n_blocks - 1, FOX_WIDTH, ROW_BLOCK), BF16),
        jax.ShapeDtypeStruct((rows, wide), BF16),
        jax.ShapeDtypeStruct((n_blocks, FOX_WIDTH, ROW_BLOCK), BF16),
        jax.ShapeDtypeStruct((rows, 3 * GDN_K_WIDTH), BF16),
        jax.ShapeDtypeStruct((rows, GDN_V_WIDTH), BF16),
        jax.ShapeDtypeStruct((rows, D_MODEL), BF16),
        jax.ShapeDtypeStruct((rows, D_MODEL), BF16),
        jax.ShapeDtypeStruct((rows, LANES), F32),
        jax.ShapeDtypeStruct((n_blocks * SUBLANES, LANES), F32),
    ]
    q_spec = pl.BlockSpec((1, FOX_WIDTH, ROW_BLOCK), lambda i: (jnp.maximum(i - 1, 0), 0, 0))
    v_spec = pl.BlockSpec((1, FOX_WIDTH, ROW_BLOCK), lambda i: (i, 0, 0))
    out_specs = [q_spec, row_spec(wide), v_spec, row_spec(3 * GDN_K_WIDTH),
                 row_spec(GDN_V_WIDTH), row_spec(D_MODEL), row_spec(D_MODEL), row_spec(LANES),
                 pl.BlockSpec((SUBLANES, LANES), lambda i: (i, 0))]
    return pl.pallas_call(
        _proj_body,
        grid=(n_blocks,),
        in_specs=in_specs,
        out_specs=out_specs,
        out_shape=out_shape,
        scratch_shapes=[pltpu.VMEM((1, LANES), F32)],
        compiler_params=pltpu.CompilerParams(dimension_semantics=("arbitrary",),
                                             vmem_limit_bytes=VMEM_LIMIT_BYTES),
        name="ln_in_proj",
    )(prefix, x, lng, lnb, wq, wk, wv, wg, wz, wga, wgb, wn, par, tri, e_mat)


def _attn_body(r_ref, qt_ref, k_ref, vt_ref, o_ref, s_scr, mx_scr, m_scr, acc_scr):
    h = pl.program_id(0)
    i = pl.program_id(1)
    q_t = jnp.concatenate([qt_ref[b] for b in range(Q_BLOCK // ROW_BLOCK)], axis=1)
    pad_row = lax.broadcasted_iota(jnp.int32, (HEAD_SLOT - FOX_HEAD_DIM, Q_BLOCK), 0)
    q_t = jnp.concatenate([q_t, jnp.where(pad_row < 3, 1.0, 0.0).astype(BF16)], axis=0)
    one_row = lax.broadcasted_iota(jnp.int32, (V_ROWS - FOX_HEAD_DIM, ROW_BLOCK), 0)
    v_ones = jnp.where(one_row == 0, 1.0, 0.0).astype(BF16)

    all_q = slice(0, Q_BLOCK)

    def produce(slot, j, mask=None, qs=all_q):
        start = pl.multiple_of(j * ROW_BLOCK, ROW_BLOCK)
        s = _dot(k_ref[pl.ds(start, ROW_BLOCK), :], q_t[:, qs])
        if mask is not None:
            s = jnp.where(mask, s, MASK_BIAS)
        s_scr[slot, :, qs] = s
        mx_scr[slot, :, qs] = jnp.max(s.reshape(ROW_BLOCK // SUBLANES, SUBLANES, qs.stop - qs.start), axis=0)

    def consume(slot, j, qs=all_q):
        rj = r_ref[j * SUBLANES + h] * LOG2E
        m = m_scr[:, qs]
        m_new = jnp.maximum(m, jnp.max(mx_scr[slot, :, qs], axis=0, keepdims=True) - rj)
        p = jnp.exp2(s_scr[slot, :, qs] - (m_new + rj))
        alpha = jnp.exp2(m - m_new)
        v_t = jnp.concatenate([vt_ref[j], v_ones], axis=0)
        acc_scr[:, qs] = alpha * acc_scr[:, qs] + _dot(v_t, p.astype(BF16))
        m_scr[:, qs] = m_new

    m_scr[...] = jnp.full(m_scr.shape, NEG_INIT, F32)
    acc_scr[...] = jnp.zeros(acc_scr.shape, F32)
    produce(0, 0)

    def pair(j):
        produce(1, j + 1)
        consume(0, j)
        produce(0, j + 2)
        consume(1, j + 1)

    def quad(t, _):
        pair(4 * t)
        pair(4 * t + 2)
        return 0

    lax.fori_loop(0, i // 2, quad, 0)

    @pl.when(lax.rem(i, 2) == 1)
    def _():
        pair(2 * i - 2)

    first_diag = 2 * i + 1
    key = lax.broadcasted_iota(jnp.int32, (ROW_BLOCK, Q_BLOCK), 0)
    qry = lax.broadcasted_iota(jnp.int32, (ROW_BLOCK, Q_BLOCK), 1)
    late_q = slice(ROW_BLOCK, Q_BLOCK)
    produce(1, first_diag, qry >= key)
    consume(0, first_diag - 1)
    produce(0, first_diag + 1, (qry >= key)[:, :ROW_BLOCK], late_q)
    consume(1, first_diag)
    consume(0, first_diag + 1, late_q)

    acc = acc_scr[...]
    denom = acc[FOX_HEAD_DIM:FOX_HEAD_DIM + 1, :]
    out_t = jnp.concatenate([acc[:FOX_HEAD_DIM] / denom,
                             jnp.zeros((HEAD_SLOT - FOX_HEAD_DIM, Q_BLOCK), F32)], axis=0)
    o_ref[...] = out_t.T.astype(o_ref.dtype)


def _attn_call(r_flat, qt, ka, vt):
    n_kv = vt.shape[0]
    n_q = qt.shape[0] * ROW_BLOCK // Q_BLOCK
    return pl.pallas_call(
        _attn_body,
        grid=(FOX_HEADS, n_q),
        in_specs=[
            pl.BlockSpec(memory_space=pltpu.SMEM),
            pl.BlockSpec((Q_BLOCK // ROW_BLOCK, FOX_HEAD_DIM, ROW_BLOCK), lambda h, i: (i, h, 0)),
            pl.BlockSpec((n_kv * ROW_BLOCK, HEAD_SLOT), lambda h, i: (0, h)),
            pl.BlockSpec((n_kv, FOX_HEAD_DIM, ROW_BLOCK), lambda h, i: (0, h, 0)),
        ],
        out_specs=pl.BlockSpec((Q_BLOCK, HEAD_SLOT), lambda h, i: (i, h)),
        out_shape=jax.ShapeDtypeStruct((n_q * Q_BLOCK, FOX_HEADS * HEAD_SLOT), BF16),
        scratch_shapes=[pltpu.VMEM((2, ROW_BLOCK, Q_BLOCK), F32),
                        pltpu.VMEM((2, SUBLANES, Q_BLOCK), F32),
                        pltpu.VMEM((1, Q_BLOCK), F32),
                        pltpu.VMEM((V_ROWS, Q_BLOCK), F32)],
        compiler_params=pltpu.CompilerParams(dimension_semantics=("arbitrary", "arbitrary"),
                                             vmem_limit_bytes=VMEM_LIMIT_BYTES),
        name="fox_attention",
    )(r_flat, qt, ka, vt)


def _gdn_body(qkv_ref, hist_ref, small_ref, z_ref, convw_ref, normw_ref, y_ref,
              state_ref, xn_ref, u_scr, w_scr, attn_scr, qd_scr, kd_scr, vnew_scr):
    i = pl.program_id(0)

    @pl.when(i == 0)
    def _():
        state_ref[...] = jnp.zeros_like(state_ref)
        vnew_scr[...] = jnp.zeros_like(vnew_scr)

    hist = jnp.where(i > 0, hist_ref[...].astype(F32), 0.0)
    cur = qkv_ref[...].astype(F32)
    ext = jnp.concatenate([hist, cur], axis=0)
    conv = convw_ref[CONV_WIDTH - 1:CONV_WIDTH, :] * cur
    for t in range(CONV_WIDTH - 1):
        delay = CONV_WIDTH - 1 - t
        shifted = pltpu.roll(ext, delay, axis=0)[SUBLANES:, :]
        conv = conv + convw_ref[t:t + 1, :] * shifted
    act = _silu(conv)
    for hd in range(2 * GDN_HEADS):
        sl = slice(hd * GDN_KEY_DIM, (hd + 1) * GDN_KEY_DIM)
        blk = act[:, sl]
        scale = lax.rsqrt(jnp.sum(blk * blk, axis=-1, keepdims=True) + NORM_EPS)
        if hd < GDN_HEADS:
            scale = scale * (GDN_KEY_DIM ** -0.5)
        xn_ref[:, sl] = blk * scale
    xn_ref[:, 2 * GDN_K_WIDTH:] = act[:, 2 * GDN_K_WIDTH:]

    ri = lax.broadcasted_iota(jnp.int32, (SUPER, SUPER), 0)
    ci = lax.broadcasted_iota(jnp.int32, (SUPER, SUPER), 1)
    same = (ri // CHUNK) == (ci // CHUNK)
    incl = jnp.logical_and(same, ri >= ci)
    strict = jnp.logical_and(same, ri > ci)
    eye = (ri == ci).astype(F32)

    def head_slices(hd):
        qs = slice(hd * GDN_KEY_DIM, (hd + 1) * GDN_KEY_DIM)
        ks = slice(GDN_K_WIDTH + hd * GDN_KEY_DIM, GDN_K_WIDTH + (hd + 1) * GDN_KEY_DIM)
        vs = slice(2 * GDN_K_WIDTH + hd * GDN_VAL_DIM, 2 * GDN_K_WIDTH + (hd + 1) * GDN_VAL_DIM)
        return qs, ks, vs

    for sc in range(ROW_BLOCK // SUPER):
        rows = slice(sc * SUPER, (sc + 1) * SUPER)
        sm = small_ref[rows, :]
        sm_t = sm.T
        for h0 in range(0, GDN_HEADS, HEAD_GROUP):
            heads = range(h0, h0 + HEAD_GROUP)
            pw, pwb, t_inv, keep = {}, {}, {}, {}
            for hd in heads:
                qs, ks, vs = head_slices(hd)
                qc = xn_ref[rows, qs]
                kc = xn_ref[rows, ks]
                vc = xn_ref[rows, vs]
                gcol = sm[:, G_LANE0 + hd:G_LANE0 + hd + 1]
                bcol = sm[:, BETA_LANE0 + hd:BETA_LANE0 + hd + 1]
                grow = sm_t[G_LANE0 + hd:G_LANE0 + hd + 1, :]
                glast = jnp.concatenate(
                    [jnp.broadcast_to(gcol[(c + 1) * CHUNK - 1:(c + 1) * CHUNK, :], (CHUNK, 1))
                     for c in range(SUPER // CHUNK)], axis=0)
                decay = jnp.exp2(LOG2E * gcol - LOG2E * grow)
                egc = jnp.exp(gcol)
                kb = kc * bcol
                kcb = kc.astype(BF16)
                pw[hd] = jnp.where(strict, _nt_dot((-kb).astype(BF16), kcb) * decay, 0.0)
                pwb[hd] = pw[hd].astype(BF16)
                t_inv[hd] = eye + pw[hd]
                attn_scr[hd, rows, :] = jnp.where(
                    incl, _nt_dot(qc.astype(BF16), kcb) * decay, 0.0).astype(BF16)
                qd_scr[rows, qs] = (qc * egc).astype(BF16)
                kd_scr[rows, qs] = (kc * jnp.exp(glast - gcol)).astype(BF16)
                keep[hd] = jnp.concatenate([vc * bcol, kb * egc], axis=1).astype(BF16)
            for _k in range(int(math.log2(CHUNK)) - 1):
                for hd in heads:
                    pwb[hd] = _dot(pwb[hd], pwb[hd]).astype(BF16)
                for hd in heads:
                    t_inv[hd] = t_inv[hd] + _dot(t_inv[hd].astype(BF16), pwb[hd])
            for hd in heads:
                qs, _, _ = head_slices(hd)
                uw = _dot(t_inv[hd].astype(BF16), keep[hd])
                u_scr[rows, qs] = uw[:, :GDN_VAL_DIM]
                w_scr[rows, qs] = uw[:, GDN_VAL_DIM:].astype(BF16)

    normw = normw_ref[...]

    def chunk_step(c, _):
        r0 = pl.multiple_of(c * CHUNK, CHUNK)
        in_sup = pl.multiple_of(lax.rem(c, SUPER // CHUNK) * CHUNK, CHUNK)
        g_tail = small_ref[pl.ds(r0 + CHUNK - SUBLANES, SUBLANES), :]
        g_end = g_tail[SUBLANES - 1:SUBLANES, :]
        states, sbs, vnbs, outs = {}, {}, {}, {}
        for hd in range(GDN_HEADS):
            qs, _, _ = head_slices(hd)
            states[hd] = state_ref[hd]
            sbs[hd] = states[hd].astype(BF16)
            v_new = u_scr[pl.ds(r0, CHUNK), qs] - _dot(w_scr[pl.ds(r0, CHUNK), qs], sbs[hd])
            vnbs[hd] = v_new.astype(BF16)
            vnew_scr[hd, pl.ds(in_sup, CHUNK), :] = vnbs[hd]
        for hd in range(GDN_HEADS):
            qs, _, _ = head_slices(hd)
            outs[hd] = (_dot(qd_scr[pl.ds(r0, CHUNK), qs], sbs[hd])
                        + _dot(attn_scr[hd, pl.ds(r0, CHUNK), :], vnew_scr[hd]))
        for hd in range(GDN_HEADS):
            qs, _, _ = head_slices(hd)
            gl = jnp.exp(g_end[:, G_LANE0 + hd:G_LANE0 + hd + 1])
            state_ref[hd] = states[hd] * gl + lax.dot_general(
                kd_scr[pl.ds(r0, CHUNK), qs], vnbs[hd], (((0,), (0,)), ((), ())),
                preferred_element_type=F32)
        for hd in range(GDN_HEADS):
            qs, _, _ = head_slices(hd)
            o = outs[hd]
            o = o * lax.rsqrt(jnp.mean(o * o, axis=-1, keepdims=True) + NORM_EPS) * normw
            zc = z_ref[pl.ds(r0, CHUNK), qs].astype(F32)
            y_ref[pl.ds(r0, CHUNK), qs] = (o * _silu(zc)).astype(y_ref.dtype)
        return 0

    lax.fori_loop(0, CHUNKS_PER_BLOCK, chunk_step, 0, unroll=2)


def _gdn_call(qkv, small, z, convw, normw):
    rows = qkv.shape[0]
    n_blocks = rows // ROW_BLOCK
    hist_blocks_per_row_block = ROW_BLOCK // SUBLANES
    return pl.pallas_call(
        _gdn_body,
        grid=(n_blocks,),
        in_specs=[
            pl.BlockSpec((ROW_BLOCK, 3 * GDN_K_WIDTH), lambda i: (i, 0)),
            pl.BlockSpec((SUBLANES, 3 * GDN_K_WIDTH),
                         lambda i: (jnp.maximum(i * hist_blocks_per_row_block - 1, 0), 0)),
            pl.BlockSpec((ROW_BLOCK, LANES), lambda i: (i, 0)),
            pl.BlockSpec((ROW_BLOCK, GDN_V_WIDTH), lambda i: (i, 0)),
            _resident(convw.shape), _resident(normw.shape),
        ],
        out_specs=pl.BlockSpec((ROW_BLOCK, GDN_V_WIDTH), lambda i: (i, 0)),
        out_shape=jax.ShapeDtypeStruct((rows, GDN_V_WIDTH), BF16),
        scratch_shapes=[pltpu.VMEM((GDN_HEADS, GDN_KEY_DIM, GDN_VAL_DIM), F32),
                        pltpu.VMEM((ROW_BLOCK, 3 * GDN_K_WIDTH), F32),
                        pltpu.VMEM((ROW_BLOCK, GDN_V_WIDTH), F32),
                        pltpu.VMEM((ROW_BLOCK, GDN_K_WIDTH), BF16),
                        pltpu.VMEM((GDN_HEADS, ROW_BLOCK, SUPER), BF16),
                        pltpu.VMEM((ROW_BLOCK, GDN_K_WIDTH), BF16),
                        pltpu.VMEM((ROW_BLOCK, GDN_K_WIDTH), BF16),
                        pltpu.VMEM((GDN_HEADS, SUPER, GDN_VAL_DIM), BF16)],
        compiler_params=pltpu.CompilerParams(dimension_semantics=("arbitrary",),
                                             vmem_limit_bytes=VMEM_LIMIT_BYTES),
        name="gated_delta_rule",
    )(qkv, qkv, small, z, convw, normw)


def _out_body(x_ref, ya_ref, yb_ref, ga_ref, gb_ref, lng_ref, lnb_ref, woa_ref, wob_ref, wo_ref,
              ln1g_ref, ln1b_ref, wgate_ref, wup_ref, wdown_ref, ln2g_ref, ln2b_ref, o_ref):
    h0 = _layer_norm(x_ref[...], lng_ref[...], lnb_ref[...])
    pa = _dot(ya_ref[...], woa_ref[...])
    pb = _dot(yb_ref[...], wob_ref[...])
    mixed = _sigmoid(ga_ref[...].astype(F32)) * pa + _sigmoid(gb_ref[...].astype(F32)) * pb
    h1 = _layer_norm(ALPHA * h0 + _dot(mixed.astype(BF16), wo_ref[...]), ln1g_ref[...], ln1b_ref[...])
    h1b = h1.astype(BF16)
    d_ff = wgate_ref.shape[1]
    half = d_ff // 2
    ffn = jnp.zeros_like(h1)
    for c in range(2):
        sl = slice(c * half, (c + 1) * half)
        gate = _dot(h1b, wgate_ref[:, sl])
        up = _dot(h1b, wup_ref[:, sl])
        ffn = ffn + _dot((_silu(gate) * up).astype(BF16), wdown_ref[sl, :])
    o_ref[...] = _layer_norm(ALPHA * h1 + ffn, ln2g_ref[...], ln2b_ref[...])


def _out_call(x, ya, yb, ga, gb, lng, lnb, woa, wob, wo, ln1g, ln1b, wgate, wup, wdown, ln2g, ln2b):
    seq = x.shape[0]
    n_blocks = seq // ROW_BLOCK
    main = lambda i: (i, 0)
    shifted = lambda i: (i + 1, 0)
    res = [lng, lnb, woa, wob, wo, ln1g, ln1b, wgate, wup, wdown, ln2g, ln2b]
    return pl.pallas_call(
        _out_body,
        grid=(n_blocks,),
        in_specs=[
            pl.BlockSpec((ROW_BLOCK, D_MODEL), main),
            pl.BlockSpec((ROW_BLOCK, ya.shape[1]), main),
            pl.BlockSpec((ROW_BLOCK, GDN_V_WIDTH), shifted),
            pl.BlockSpec((ROW_BLOCK, D_MODEL), shifted),
            pl.BlockSpec((ROW_BLOCK, D_MODEL), shifted),
        ] + [_resident(a.shape) for a in res],
        out_specs=pl.BlockSpec((ROW_BLOCK, D_MODEL), main),
        out_shape=jax.ShapeDtypeStruct((seq, D_MODEL), F32),
        compiler_params=pltpu.CompilerParams(dimension_semantics=("parallel",),
                                             vmem_limit_bytes=VMEM_LIMIT_BYTES),
        name="merge_out_ffn",
    )(x, ya, yb, ga, gb, *res)


def _slot_rows(w):
    d = w.shape[1]
    w = w.reshape(FOX_HEADS, FOX_HEAD_DIM, d)
    w = jnp.pad(w, ((0, 0), (0, HEAD_SLOT - FOX_HEAD_DIM), (0, 0)))
    return w.reshape(FOX_HEADS * HEAD_SLOT, d)


def _constants():
    wide = FOX_HEADS * HEAD_SLOT
    r = np.arange(ROW_BLOCK)
    tri_full = (r[:, None] >= r[None, :])
    tri_chunk = tri_full & ((r[:, None] // CHUNK) == (r[None, :] // CHUNK))
    tri = np.concatenate([tri_full, tri_chunk], axis=0).astype(np.float32)
    e_mat = np.zeros((3 * LANES, wide), np.float32)
    for hd in range(FOX_HEADS):
        for piece in range(3):
            e_mat[piece * LANES + LOGF_LANE0 + hd, hd * HEAD_SLOT + BIAS_LANE + piece] = 1.0
    return jnp.asarray(tri, BF16), jnp.asarray(e_mat, BF16)


def kernel(x, meta_tokens, ln_in_g, ln_in_b, w_in, b_f, conv_w, a_log, dt_bias, gdn_norm_w,
           w_out_a, w_out_b, w_o, ln1_g, ln1_b, w_gate, w_up, w_down, ln2_g, ln2_b):
    bsz, seq, d = x.shape
    assert bsz == 1 and d == D_MODEL and seq % Q_BLOCK == 0 and w_in.shape[0] == DEPTH
    x2 = x[0]
    wt = jnp.swapaxes(w_in[0], 0, 1)
    o = 0
    part = {}
    for name, width in (("q_a", FOX_WIDTH), ("k_a", FOX_WIDTH), ("v_a", FOX_WIDTH), ("f_a", FOX_HEADS),
                        ("qkv_b", 3 * GDN_K_WIDTH), ("a_b", GDN_HEADS), ("b_b", GDN_HEADS),
                        ("z_b", GDN_V_WIDTH), ("gate_a", D_MODEL), ("gate_b", D_MODEL)):
        part[name] = wt[o:o + width]
        o += width
    assert o == wt.shape[0]
    cols = {name: part[name].astype(BF16) for name in ("k_a", "qkv_b", "z_b", "gate_a", "gate_b")}

    wq = (part["q_a"] * (FOX_HEAD_DIM ** -0.5 * LOG2E)).astype(BF16)
    wk = _slot_rows(cols["k_a"])
    wv = part["v_a"].astype(BF16)
    wn = jnp.zeros((LANES, d), F32)
    wn = wn.at[LOGF_LANE0:LOGF_LANE0 + FOX_HEADS].set(part["f_a"])
    wn = wn.at[G_LANE0:G_LANE0 + GDN_HEADS].set(part["a_b"])
    wn = wn.at[BETA_LANE0:BETA_LANE0 + GDN_HEADS].set(part["b_b"]).astype(BF16)
    par = jnp.zeros((SUBLANES, LANES), F32)
    par = par.at[0, LOGF_LANE0:LOGF_LANE0 + FOX_HEADS].set(b_f[0])
    par = par.at[1, G_LANE0:G_LANE0 + GDN_HEADS].set(a_log[0])
    par = par.at[2, G_LANE0:G_LANE0 + GDN_HEADS].set(dt_bias[0])
    tri, e_mat = _constants()
    prefix = jnp.zeros((PRE, d), F32).at[LEAD:].set(meta_tokens.astype(F32))
    row = lambda v: v.reshape(1, -1).astype(F32)

    qt, ka, vt, qkv, z, ga, gb, small, r_blk = _proj_call(
        prefix, x2, row(ln_in_g), row(ln_in_b), wq, wk, wv, cols["qkv_b"],
        cols["z_b"], cols["gate_a"], cols["gate_b"],
        wn, par, tri, e_mat)

    r_flat = r_blk[:, :SUBLANES].reshape(-1, SUBLANES, SUBLANES)[:, 0, :].reshape(-1)
    ya = _attn_call(r_flat, qt, ka, vt)
    yb = _gdn_call(qkv, small, z, conv_w[0].astype(F32), row(gdn_norm_w[0]))
    out = _out_call(x2, ya, yb, ga, gb, row(ln_in_g), row(ln_in_b),
                    _slot_rows(w_out_a[0]).astype(BF16), w_out_b[0].astype(BF16), w_o[0].astype(BF16),
                    row(ln1_g[0]), row(ln1_b[0]), w_gate[0].astype(BF16), w_up[0].astype(BF16),
                    w_down[0].astype(BF16), row(ln2_g[0]), row(ln2_b[0]))
    return out[None]
```

```python
import functools
import math

import jax
import jax.numpy as jnp
import numpy as np
from jax import lax
from jax.experimental import pallas as pl
from jax.experimental.pallas import tpu as pltpu

D_MODEL = 1024
CHUNK = 64
N_META = 16
FOX_HEADS = 8
FOX_HEAD_DIM = 64
FOX_WIDTH = FOX_HEADS * FOX_HEAD_DIM
GDN_HEADS = 8
GDN_KEY_DIM = 128
GDN_VAL_DIM = 128
GDN_K_WIDTH = GDN_HEADS * GDN_KEY_DIM
GDN_V_WIDTH = GDN_HEADS * GDN_VAL_DIM
CONV_WIDTH = 4
DEPTH = 1
ALPHA = (2.0 * DEPTH) ** 0.25
LN_EPS = 1e-5
NORM_EPS = 1e-6

LANES = 128
SUBLANES = 8
VMEM_LIMIT_BYTES = 56 * 1024 * 1024

ROW_BLOCK = 512
PRE = ROW_BLOCK
LEAD = PRE - N_META
CHUNKS_PER_BLOCK = ROW_BLOCK // CHUNK
Q_BLOCK = 2 * ROW_BLOCK
SUPER = 128
HEAD_GROUP = 8
HEAD_SLOT = LANES
BIAS_LANE = FOX_HEAD_DIM
V_ROWS = FOX_HEAD_DIM + 16
MASK_BIAS = -1e30
NEG_INIT = -1e30
LOG2E = math.log2(math.e)
F32 = jnp.float32
BF16 = jnp.bfloat16

LOGF_LANE0 = 0
G_LANE0 = 8
BETA_LANE0 = 16


def _nt_dot(a, b):
    return lax.dot_general(a, b, (((1,), (1,)), ((), ())), preferred_element_type=F32)


def _dot(a, b):
    return jnp.dot(a, b, preferred_element_type=F32)


def _layer_norm(x, g, b):
    mu = jnp.mean(x, axis=-1, keepdims=True)
    xc = x - mu
    var = jnp.mean(xc * xc, axis=-1, keepdims=True)
    return xc * lax.rsqrt(var + LN_EPS) * g + b


def _softplus(x):
    return jnp.maximum(x, 0.0) + jnp.log(1.0 + jnp.exp(-jnp.abs(x)))


def _sigmoid(x):
    return 1.0 / (1.0 + jnp.exp(-x))


def _silu(x):
    return x * _sigmoid(x)


def _split3(x):
    hi = x.astype(BF16)
    r1 = x - hi.astype(F32)
    mid = r1.astype(BF16)
    lo = (r1 - mid.astype(F32)).astype(BF16)
    return hi, mid, lo


def _resident(shape):
    nd = len(shape)
    return pl.BlockSpec(shape, lambda *_: (0,) * nd, pipeline_mode=pl.Buffered(1))


def _proj_body(prefix_ref, x_ref, lng_ref, lnb_ref, wq_ref, wk_ref, wv_ref, wg_ref, wz_ref, wga_ref,
               wgb_ref, wn_ref, par_ref, tri_ref, e_ref,
               qt_ref, ka_ref, vt_ref, qkv_ref, z_ref, ga_ref, gb_ref, small_ref, r_ref,
               carry_ref):
    i = pl.program_id(0)

    @pl.when(i == 0)
    def _():
        carry_ref[...] = jnp.zeros_like(carry_ref)

    src = jnp.where(i == 0, prefix_ref[...], x_ref[...])
    h = _layer_norm(src, lng_ref[...], lnb_ref[...])
    row = lax.broadcasted_iota(jnp.int32, (ROW_BLOCK, 1), 0)
    valid = jnp.logical_or(i > 0, row >= LEAD)
    hb = jnp.where(valid, h, 0.0).astype(BF16)

    nar = _nt_dot(hb, wn_ref[...])
    lane = lax.broadcasted_iota(jnp.int32, (1, LANES), 1)
    b_f = par_ref[0:1, :]
    a_log = par_ref[1:2, :]
    dt_bias = par_ref[2:3, :]
    xf = nar + b_f
    logf = jnp.minimum(xf, 0.0) - jnp.log(1.0 + jnp.exp(-jnp.abs(xf)))
    g = -jnp.exp(a_log) * _softplus(nar + dt_bias)
    beta = _sigmoid(nar)
    is_logf = lane < G_LANE0
    is_g = jnp.logical_and(lane >= G_LANE0, lane < BETA_LANE0)
    vals = jnp.where(is_logf, logf, jnp.where(is_g, g, 0.0))

    hi, mid, lo = _split3(vals)
    tri = tri_ref[...]
    cs = _dot(tri, hi) + _dot(tri, mid) + _dot(tri, lo)
    d_blk = cs[:ROW_BLOCK]
    gc = cs[ROW_BLOCK:]
    carry = carry_ref[...]
    r_ref[...] = jnp.broadcast_to(carry, r_ref.shape)
    carry_ref[...] = carry + d_blk[ROW_BLOCK - 1:ROW_BLOCK, :]
    small_ref[...] = jnp.where(is_g, gc, jnp.where(lane >= BETA_LANE0, beta, d_blk))

    nd = jnp.where(valid, -LOG2E * d_blk, MASK_BIAS)
    nd = jnp.where(is_logf, nd, 0.0)
    phi, pmid, plo = _split3(nd)
    pieces = jnp.concatenate([phi, pmid, plo], axis=1)
    qt_ref[0] = _nt_dot(wq_ref[...], hb).astype(BF16)
    ka_ref[...] = (_nt_dot(hb, wk_ref[...]) + _dot(pieces, e_ref[...])).astype(BF16)
    vt_ref[0] = _nt_dot(wv_ref[...], hb).astype(BF16)

    for c in range(3):
        sl = slice(c * GDN_K_WIDTH, (c + 1) * GDN_K_WIDTH)
        qkv_ref[:, sl] = _nt_dot(hb, wg_ref[sl, :]).astype(BF16)
    z_ref[...] = _nt_dot(hb, wz_ref[...]).astype(BF16)
    ga_ref[...] = _nt_dot(hb, wga_ref[...]).astype(BF16)
    gb_ref[...] = _nt_dot(hb, wgb_ref[...]).astype(BF16)


def _proj_call(prefix, x, lng, lnb, wq, wk, wv, wg, wz, wga, wgb, wn, par, tri, e_mat):
    seq = x.shape[0]
    n_blocks = seq // ROW_BLOCK + 1
    rows = n_blocks * ROW_BLOCK
    wide = FOX_HEADS * HEAD_SLOT

    def row_spec(width):
        return pl.BlockSpec((ROW_BLOCK, width), lambda i: (i, 0))

    in_specs = [
        _resident(prefix.shape),
        pl.BlockSpec((ROW_BLOCK, D_MODEL), lambda i: (jnp.maximum(i - 1, 0), 0)),
        _resident(lng.shape), _resident(lnb.shape),
        _resident(wq.shape), _resident(wk.shape), _resident(wv.shape), _resident(wg.shape),
        _resident(wz.shape), _resident(wga.shape), _resident(wgb.shape), _resident(wn.shape),
        _resident(par.shape), _resident(tri.shape), _resident(e_mat.shape),
    ]
    out_shape = [
        jax.ShapeDtypeStruct((n_blocks - 1, FOX_WIDTH, ROW_BLOCK), BF16),
        jax.ShapeDtypeStruct((rows, wide), BF16),
        jax.ShapeDtypeStruct((n_blocks, FOX_WIDTH, ROW_BLOCK), BF16),
        jax.ShapeDtypeStruct((rows, 3 * GDN_K_WIDTH), BF16),
        jax.ShapeDtypeStruct((rows, GDN_V_WIDTH), BF16),
        jax.ShapeDtypeStruct((rows, D_MODEL), BF16),
        jax.ShapeDtypeStruct((rows, D_MODEL), BF16),
        jax.ShapeDtypeStruct((rows, LANES), F32),
        jax.ShapeDtypeStruct((n_blocks * SUBLANES, LANES), F32),
    ]
    q_spec = pl.BlockSpec((1, FOX_WIDTH, ROW_BLOCK), lambda i: (jnp.maximum(i - 1, 0), 0, 0))
    v_spec = pl.BlockSpec((1, FOX_WIDTH, ROW_BLOCK), lambda i: (i, 0, 0))
    out_specs = [q_spec, row_spec(wide), v_spec, row_spec(3 * GDN_K_WIDTH),
                 row_spec(GDN_V_WIDTH), row_spec(D_MODEL), row_spec(D_MODEL), row_spec(LANES),
                 pl.BlockSpec((SUBLANES, LANES), lambda i: (i, 0))]
    return pl.pallas_call(
        _proj_body,
        grid=(n_blocks,),
        in_specs=in_specs,
        out_specs=out_specs,
        out_shape=out_shape,
        scratch_shapes=[pltpu.VMEM((1, LANES), F32)],
        compiler_params=pltpu.CompilerParams(dimension_semantics=("arbitrary",),
                                             vmem_limit_bytes=VMEM_LIMIT_BYTES),
        name="ln_in_proj",
    )(prefix, x, lng, lnb, wq, wk, wv, wg, wz, wga, wgb, wn, par, tri, e_mat)


def _attn_body(r_ref, qt_ref, k_ref, vt_ref, o_ref, s_scr, mx_scr, m_scr, acc_scr):
    h = pl.program_id(0)
    i = pl.program_id(1)
    q_t = jnp.concatenate([qt_ref[b] for b in range(Q_BLOCK // ROW_BLOCK)], axis=1)
    pad_row = lax.broadcasted_iota(jnp.int32, (HEAD_SLOT - FOX_HEAD_DIM, Q_BLOCK), 0)
    q_t = jnp.concatenate([q_t, jnp.where(pad_row < 3, 1.0, 0.0).astype(BF16)], axis=0)
    one_row = lax.broadcasted_iota(jnp.int32, (V_ROWS - FOX_HEAD_DIM, ROW_BLOCK), 0)
    v_ones = jnp.where(one_row == 0, 1.0, 0.0).astype(BF16)

    all_q = slice(0, Q_BLOCK)

    def produce(slot, j, mask=None, qs=all_q):
        start = pl.multiple_of(j * ROW_BLOCK, ROW_BLOCK)
        s = _dot(k_ref[pl.ds(start, ROW_BLOCK), :], q_t[:, qs])
        if mask is not None:
            s = jnp.where(mask, s, MASK_BIAS)
        s_scr[slot, :, qs] = s
        mx_scr[slot, :, qs] = jnp.max(s.reshape(ROW_BLOCK // SUBLANES, SUBLANES, qs.stop - qs.start), axis=0)

    def consume(slot, j, qs=all_q):
        rj = r_ref[j * SUBLANES + h] * LOG2E
        m = m_scr[:, qs]
        m_new = jnp.maximum(m, jnp.max(mx_scr[slot, :, qs], axis=0, keepdims=True) - rj)
        p = jnp.exp2(s_scr[slot, :, qs] - (m_new + rj))
        alpha = jnp.exp2(m - m_new)
        v_t = jnp.concatenate([vt_ref[j], v_ones], axis=0)
        acc_scr[:, qs] = alpha * acc_scr[:, qs] + _dot(v_t, p.astype(BF16))
        m_scr[:, qs] = m_new

    m_scr[...] = jnp.full(m_scr.shape, NEG_INIT, F32)
    acc_scr[...] = jnp.zeros(acc_scr.shape, F32)
    produce(0, 0)

    def pair(j):
        produce(1, j + 1)
        consume(0, j)
        produce(0, j + 2)
        consume(1, j + 1)

    def quad(t, _):
        pair(4 * t)
        pair(4 * t + 2)
        return 0

    lax.fori_loop(0, i // 2, quad, 0)

    @pl.when(lax.rem(i, 2) == 1)
    def _():
        pair(2 * i - 2)

    first_diag = 2 * i + 1
    key = lax.broadcasted_iota(jnp.int32, (ROW_BLOCK, Q_BLOCK), 0)
    qry = lax.broadcasted_iota(jnp.int32, (ROW_BLOCK, Q_BLOCK), 1)
    late_q = slice(ROW_BLOCK, Q_BLOCK)
    produce(1, first_diag, qry >= key)
    consume(0, first_diag - 1)
    produce(0, first_diag + 1, (qry >= key)[:, :ROW_BLOCK], late_q)
    consume(1, first_diag)
    consume(0, first_diag + 1, late_q)

    acc = acc_scr[...]
    denom = acc[FOX_HEAD_DIM:FOX_HEAD_DIM + 1, :]
    out_t = jnp.concatenate([acc[:FOX_HEAD_DIM] / denom,
                             jnp.zeros((HEAD_SLOT - FOX_HEAD_DIM, Q_BLOCK), F32)], axis=0)
    o_ref[...] = out_t.T.astype(o_ref.dtype)


def _attn_call(r_flat, qt, ka, vt):
    n_kv = vt.shape[0]
    n_q = qt.shape[0] * ROW_BLOCK // Q_BLOCK
    return pl.pallas_call(
        _attn_body,
        grid=(FOX_HEADS, n_q),
        in_specs=[
            pl.BlockSpec(memory_space=pltpu.SMEM),
            pl.BlockSpec((Q_BLOCK // ROW_BLOCK, FOX_HEAD_DIM, ROW_BLOCK), lambda h, i: (i, h, 0)),
            pl.BlockSpec((n_kv * ROW_BLOCK, HEAD_SLOT), lambda h, i: (0, h)),
            pl.BlockSpec((n_kv, FOX_HEAD_DIM, ROW_BLOCK), lambda h, i: (0, h, 0)),
        ],
        out_specs=pl.BlockSpec((Q_BLOCK, HEAD_SLOT), lambda h, i: (i, h)),
        out_shape=jax.ShapeDtypeStruct((n_q * Q_BLOCK, FOX_HEADS * HEAD_SLOT), BF16),
        scratch_shapes=[pltpu.VMEM((2, ROW_BLOCK, Q_BLOCK), F32),
                        pltpu.VMEM((2, SUBLANES, Q_BLOCK), F32),
                        pltpu.VMEM((1, Q_BLOCK), F32),
                        pltpu.VMEM((V_ROWS, Q_BLOCK), F32)],
        compiler_params=pltpu.CompilerParams(dimension_semantics=("arbitrary", "arbitrary"),
                                             vmem_limit_bytes=VMEM_LIMIT_BYTES),
        name="fox_attention",
    )(r_flat, qt, ka, vt)


def _gdn_body(qkv_ref, hist_ref, small_ref, z_ref, convw_ref, normw_ref, y_ref,
              state_ref, xn_ref, u_scr, w_scr, attn_scr, qd_scr, kd_scr, vnew_scr):
    i = pl.program_id(0)

    @pl.when(i == 0)
    def _():
        state_ref[...] = jnp.zeros_like(state_ref)
        vnew_scr[...] = jnp.zeros_like(vnew_scr)

    hist = jnp.where(i > 0, hist_ref[...].astype(F32), 0.0)
    cur = qkv_ref[...].astype(F32)
    ext = jnp.concatenate([hist, cur], axis=0)
    conv = convw_ref[CONV_WIDTH - 1:CONV_WIDTH, :] * cur
    for t in range(CONV_WIDTH - 1):
        delay = CONV_WIDTH - 1 - t
        shifted = pltpu.roll(ext, delay, axis=0)[SUBLANES:, :]
        conv = conv + convw_ref[t:t + 1, :] * shifted
    act = _silu(conv)
    for hd in range(2 * GDN_HEADS):
        sl = slice(hd * GDN_KEY_DIM, (hd + 1) * GDN_KEY_DIM)
        blk = act[:, sl]
        scale = lax.rsqrt(jnp.sum(blk * blk, axis=-1, keepdims=True) + NORM_EPS)
        if hd < GDN_HEADS:
            scale = scale * (GDN_KEY_DIM ** -0.5)
        xn_ref[:, sl] = blk * scale
    xn_ref[:, 2 * GDN_K_WIDTH:] = act[:, 2 * GDN_K_WIDTH:]

    ri = lax.broadcasted_iota(jnp.int32, (SUPER, SUPER), 0)
    ci = lax.broadcasted_iota(jnp.int32, (SUPER, SUPER), 1)
    same = (ri // CHUNK) == (ci // CHUNK)
    incl = jnp.logical_and(same, ri >= ci)
    strict = jnp.logical_and(same, ri > ci)
    eye = (ri == ci).astype(F32)

    def head_slices(hd):
        qs = slice(hd * GDN_KEY_DIM, (hd + 1) * GDN_KEY_DIM)
        ks = slice(GDN_K_WIDTH + hd * GDN_KEY_DIM, GDN_K_WIDTH + (hd + 1) * GDN_KEY_DIM)
        vs = slice(2 * GDN_K_WIDTH + hd * GDN_VAL_DIM, 2 * GDN_K_WIDTH + (hd + 1) * GDN_VAL_DIM)
        return qs, ks, vs

    for sc in range(ROW_BLOCK // SUPER):
        rows = slice(sc * SUPER, (sc + 1) * SUPER)
        sm = small_ref[rows, :]
        sm_t = sm.T
        for h0 in range(0, GDN_HEADS, HEAD_GROUP):
            heads = range(h0, h0 + HEAD_GROUP)
            pw, pwb, t_inv, keep = {}, {}, {}, {}
            for hd in heads:
                qs, ks, vs = head_slices(hd)
                qc = xn_ref[rows, qs]
                kc = xn_ref[rows, ks]
                vc = xn_ref[rows, vs]
                gcol = sm[:, G_LANE0 + hd:G_LANE0 + hd + 1]
                bcol = sm[:, BETA_LANE0 + hd:BETA_LANE0 + hd + 1]
                grow = sm_t[G_LANE0 + hd:G_LANE0 + hd + 1, :]
                glast = jnp.concatenate(
                    [jnp.broadcast_to(gcol[(c + 1) * CHUNK - 1:(c + 1) * CHUNK, :], (CHUNK, 1))
                     for c in range(SUPER // CHUNK)], axis=0)
                decay = jnp.exp2(LOG2E * gcol - LOG2E * grow)
                egc = jnp.exp(gcol)
                kb = kc * bcol
                kcb = kc.astype(BF16)
                pw[hd] = jnp.where(strict, _nt_dot((-kb).astype(BF16), kcb) * decay, 0.0)
                pwb[hd] = pw[hd].astype(BF16)
                t_inv[hd] = eye + pw[hd]
                attn_scr[hd, rows, :] = jnp.where(
                    incl, _nt_dot(qc.astype(BF16), kcb) * decay, 0.0).astype(BF16)
                qd_scr[rows, qs] = (qc * egc).astype(BF16)
                kd_scr[rows, qs] = (kc * jnp.exp(glast - gcol)).astype(BF16)
                keep[hd] = jnp.concatenate([vc * bcol, kb * egc], axis=1).astype(BF16)
            for _k in range(int(math.log2(CHUNK)) - 1):
                for hd in heads:
                    pwb[hd] = _dot(pwb[hd], pwb[hd]).astype(BF16)
                for hd in heads:
                    t_inv[hd] = t_inv[hd] + _dot(t_inv[hd].astype(BF16), pwb[hd])
            for hd in heads:
                qs, _, _ = head_slices(hd)
                uw = _dot(t_inv[hd].astype(BF16), keep[hd])
                u_scr[rows, qs] = uw[:, :GDN_VAL_DIM]
                w_scr[rows, qs] = uw[:, GDN_VAL_DIM:].astype(BF16)

    normw = normw_ref[...]

    def chunk_step(c, _):
        r0 = pl.multiple_of(c * CHUNK, CHUNK)
        in_sup = pl.multiple_of(lax.rem(c, SUPER // CHUNK) * CHUNK, CHUNK)
        g_tail = small_ref[pl.ds(r0 + CHUNK - SUBLANES, SUBLANES), :]
        g_end = g_tail[SUBLANES - 1:SUBLANES, :]
        states, sbs, vnbs, outs = {}, {}, {}, {}
        for hd in range(GDN_HEADS):
            qs, _, _ = head_slices(hd)
            states[hd] = state_ref[hd]
            sbs[hd] = states[hd].astype(BF16)
            v_new = u_scr[pl.ds(r0, CHUNK), qs] - _dot(w_scr[pl.ds(r0, CHUNK), qs], sbs[hd])
            vnbs[hd] = v_new.astype(BF16)
            vnew_scr[hd, pl.ds(in_sup, CHUNK), :] = vnbs[hd]
        for hd in range(GDN_HEADS):
            qs, _, _ = head_slices(hd)
            outs[hd] = (_dot(qd_scr[pl.ds(r0, CHUNK), qs], sbs[hd])
                        + _dot(attn_scr[hd, pl.ds(r0, CHUNK), :], vnew_scr[hd]))
        for hd in range(GDN_HEADS):
            qs, _, _ = head_slices(hd)
            gl = jnp.exp(g_end[:, G_LANE0 + hd:G_LANE0 + hd + 1])
            state_ref[hd] = states[hd] * gl + lax.dot_general(
                kd_scr[pl.ds(r0, CHUNK), qs], vnbs[hd], (((0,), (0,)), ((), ())),
                preferred_element_type=F32)
        for hd in range(GDN_HEADS):
            qs, _, _ = head_slices(hd)
            o = outs[hd]
            o = o * lax.rsqrt(jnp.mean(o * o, axis=-1, keepdims=True) + NORM_EPS) * normw
            zc = z_ref[pl.ds(r0, CHUNK), qs].astype(F32)
            y_ref[pl.ds(r0, CHUNK), qs] = (o * _silu(zc)).astype(y_ref.dtype)
        return 0

    lax.fori_loop(0, CHUNKS_PER_BLOCK, chunk_step, 0, unroll=2)


def _gdn_call(qkv, small, z, convw, normw):
    rows = qkv.shape[0]
    n_blocks = rows // ROW_BLOCK
    hist_blocks_per_row_block = ROW_BLOCK // SUBLANES
    return pl.pallas_call(
        _gdn_body,
        grid=(n_blocks,),
        in_specs=[
            pl.BlockSpec((ROW_BLOCK, 3 * GDN_K_WIDTH), lambda i: (i, 0)),
            pl.BlockSpec((SUBLANES, 3 * GDN_K_WIDTH),
                         lambda i: (jnp.maximum(i * hist_blocks_per_row_block - 1, 0), 0)),
            pl.BlockSpec((ROW_BLOCK, LANES), lambda i: (i, 0)),
            pl.BlockSpec((ROW_BLOCK, GDN_V_WIDTH), lambda i: (i, 0)),
            _resident(convw.shape), _resident(normw.shape),
        ],
        out_specs=pl.BlockSpec((ROW_BLOCK, GDN_V_WIDTH), lambda i: (i, 0)),
        out_shape=jax.ShapeDtypeStruct((rows, GDN_V_WIDTH), BF16),
        scratch_shapes=[pltpu.VMEM((GDN_HEADS, GDN_KEY_DIM, GDN_VAL_DIM), F32),
                        pltpu.VMEM((ROW_BLOCK, 3 * GDN_K_WIDTH), F32),
                        pltpu.VMEM((ROW_BLOCK, GDN_V_WIDTH), F32),
                        pltpu.VMEM((ROW_BLOCK, GDN_K_WIDTH), BF16),
                        pltpu.VMEM((GDN_HEADS, ROW_BLOCK, SUPER), BF16),
                        pltpu.VMEM((ROW_BLOCK, GDN_K_WIDTH), BF16),
                        pltpu.VMEM((ROW_BLOCK, GDN_K_WIDTH), BF16),
                        pltpu.VMEM((GDN_HEADS, SUPER, GDN_VAL_DIM), BF16)],
        compiler_params=pltpu.CompilerParams(dimension_semantics=("arbitrary",),
                                             vmem_limit_bytes=VMEM_LIMIT_BYTES),
        name="gated_delta_rule",
    )(qkv, qkv, small, z, convw, normw)


def _out_body(x_ref, ya_ref, yb_ref, ga_ref, gb_ref, lng_ref, lnb_ref, woa_ref, wob_ref, wo_ref,
              ln1g_ref, ln1b_ref, wgate_ref, wup_ref, wdown_ref, ln2g_ref, ln2b_ref, o_ref):
    h0 = _layer_norm(x_ref[...], lng_ref[...], lnb_ref[...])
    pa = _dot(ya_ref[...], woa_ref[...])
    pb = _dot(yb_ref[...], wob_ref[...])
    mixed = _sigmoid(ga_ref[...].astype(F32)) * pa + _sigmoid(gb_ref[...].astype(F32)) * pb
    h1 = _layer_norm(ALPHA * h0 + _dot(mixed.astype(BF16), wo_ref[...]), ln1g_ref[...], ln1b_ref[...])
    h1b = h1.astype(BF16)
    d_ff = wgate_ref.shape[1]
    half = d_ff // 2
    ffn = jnp.zeros_like(h1)
    for c in range(2):
        sl = slice(c * half, (c + 1) * half)
        gate = _dot(h1b, wgate_ref[:, sl])
        up = _dot(h1b, wup_ref[:, sl])
        ffn = ffn + _dot((_silu(gate) * up).astype(BF16), wdown_ref[sl, :])
    o_ref[...] = _layer_norm(ALPHA * h1 + ffn, ln2g_ref[...], ln2b_ref[...])


def _out_call(x, ya, yb, ga, gb, lng, lnb, woa, wob, wo, ln1g, ln1b, wgate, wup, wdown, ln2g, ln2b):
    seq = x.shape[0]
    n_blocks = seq // ROW_BLOCK
    main = lambda i: (i, 0)
    shifted = lambda i: (i + 1, 0)
    res = [lng, lnb, woa, wob, wo, ln1g, ln1b, wgate, wup, wdown, ln2g, ln2b]
    return pl.pallas_call(
        _out_body,
        grid=(n_blocks,),
        in_specs=[
            pl.BlockSpec((ROW_BLOCK, D_MODEL), main),
            pl.BlockSpec((ROW_BLOCK, ya.shape[1]), main),
            pl.BlockSpec((ROW_BLOCK, GDN_V_WIDTH), shifted),
            pl.BlockSpec((ROW_BLOCK, D_MODEL), shifted),
            pl.BlockSpec((ROW_BLOCK, D_MODEL), shifted),
        ] + [_resident(a.shape) for a in res],
        out_specs=pl.BlockSpec((ROW_BLOCK, D_MODEL), main),
        out_shape=jax.ShapeDtypeStruct((seq, D_MODEL), F32),
        compiler_params=pltpu.CompilerParams(dimension_semantics=("parallel",),
                                             vmem_limit_bytes=VMEM_LIMIT_BYTES),
        name="merge_out_ffn",
    )(x, ya, yb, ga, gb, *res)


def _slot_rows(w):
    d = w.shape[1]
    w = w.reshape(FOX_HEADS, FOX_HEAD_DIM, d)
    w = jnp.pad(w, ((0, 0), (0, HEAD_SLOT - FOX_HEAD_DIM), (0, 0)))
    return w.reshape(FOX_HEADS * HEAD_SLOT, d)


def _constants():
    wide = FOX_HEADS * HEAD_SLOT
    r = np.arange(ROW_BLOCK)
    tri_full = (r[:, None] >= r[None, :])
    tri_chunk = tri_full & ((r[:, None] // CHUNK) == (r[None, :] // CHUNK))
    tri = np.concatenate([tri_full, tri_chunk], axis=0).astype(np.float32)
    e_mat = np.zeros((3 * LANES, wide), np.float32)
    for hd in range(FOX_HEADS):
        for piece in range(3):
            e_mat[piece * LANES + LOGF_LANE0 + hd, hd * HEAD_SLOT + BIAS_LANE + piece] = 1.0
    return jnp.asarray(tri, BF16), jnp.asarray(e_mat, BF16)


def kernel(x, meta_tokens, ln_in_g, ln_in_b, w_in, b_f, conv_w, a_log, dt_bias, gdn_norm_w,
           w_out_a, w_out_b, w_o, ln1_g, ln1_b, w_gate, w_up, w_down, ln2_g, ln2_b):
    bsz, seq, d = x.shape
    assert bsz == 1 and d == D_MODEL and seq % Q_BLOCK == 0 and w_in.shape[0] == DEPTH
    x2 = x[0]
    wt = jnp.swapaxes(w_in[0], 0, 1)
    o = 0
    part = {}
    for name, width in (("q_a", FOX_WIDTH), ("k_a", FOX_WIDTH), ("v_a", FOX_WIDTH), ("f_a", FOX_HEADS),
                        ("qkv_b", 3 * GDN_K_WIDTH), ("a_b", GDN_HEADS), ("b_b", GDN_HEADS),
                        ("z_b", GDN_V_WIDTH), ("gate_a", D_MODEL), ("gate_b", D_MODEL)):
        part[name] = wt[o:o + width]
        o += width
    assert o == wt.shape[0]
    cols = {name: part[name].astype(BF16) for name in ("k_a", "qkv_b", "z_b", "gate_a", "gate_b")}

    wq = (part["q_a"] * (FOX_HEAD_DIM ** -0.5 * LOG2E)).astype(BF16)
    wk = _slot_rows(cols["k_a"])
    wv = part["v_a"].astype(BF16)
    wn = jnp.zeros((LANES, d), F32)
    wn = wn.at[LOGF_LANE0:LOGF_LANE0 + FOX_HEADS].set(part["f_a"])
    wn = wn.at[G_LANE0:G_LANE0 + GDN_HEADS].set(part["a_b"])
    wn = wn.at[BETA_LANE0:BETA_LANE0 + GDN_HEADS].set(part["b_b"]).astype(BF16)
    par = jnp.zeros((SUBLANES, LANES), F32)
    par = par.at[0, LOGF_LANE0:LOGF_LANE0 + FOX_HEADS].set(b_f[0])
    par = par.at[1, G_LANE0:G_LANE0 + GDN_HEADS].set(a_log[0])
    par = par.at[2, G_LANE0:G_LANE0 + GDN_HEADS].set(dt_bias[0])
    tri, e_mat = _constants()
    prefix = jnp.zeros((PRE, d), F32).at[LEAD:].set(meta_tokens.astype(F32))
    row = lambda v: v.reshape(1, -1).astype(F32)

    qt, ka, vt, qkv, z, ga, gb, small, r_blk = _proj_call(
        prefix, x2, row(ln_in_g), row(ln_in_b), wq, wk, wv, cols["qkv_b"],
        cols["z_b"], cols["gate_a"], cols["gate_b"],
        wn, par, tri, e_mat)

    r_flat = r_blk[:, :SUBLANES].reshape(-1, SUBLANES, SUBLANES)[:, 0, :].reshape(-1)
    ya = _attn_call(r_flat, qt, ka, vt)
    yb = _gdn_call(qkv, small, z, conv_w[0].astype(F32), row(gdn_norm_w[0]))
    out = _out_call(x2, ya, yb, ga, gb, row(ln_in_g), row(ln_in_b),
                    _slot_rows(w_out_a[0]).astype(BF16), w_out_b[0].astype(BF16), w_o[0].astype(BF16),
                    row(ln1_g[0]), row(ln1_b[0]), w_gate[0].astype(BF16), w_up[0].astype(BF16),
                    w_down[0].astype(BF16), row(ln2_g[0]), row(ln2_b[0]))
    return out[None]
```
